```python
import math
import jax, jax.numpy as jnp
from jax import lax
import numpy as np


D_MODEL = 1024
BATCH = 8
SEQ = 4096
DEPTH = 2
DEC_BATCH = 32
DEC_SEQ = 32
PAST_LEN = 2048

CHUNK = 64
D_MIX = 2 * D_MODEL
D_A = D_MIX // 4
D_B = D_MIX // 2
D_C = D_MIX // 4
CONV_A_W = 3
SSM_HEAD_DIM = 64
SSM_HEADS = D_B // SSM_HEAD_DIM
SSM_GROUPS = 2
SSM_STATE = 128
CONV_B_W = 4
D_XBC = D_B + 2 * SSM_GROUPS * SSM_STATE
SSD_CHUNK = CHUNK
CONV_C_W = 31
D_FF = ((8 * D_MODEL // 3 + 127) // 128) * 128
CONV_F_W = 3
D_IN = 3 * D_A + D_B + D_XBC + SSM_HEADS + 2 * D_C
EPS = 1e-5

kernel_name = 'hybrid_stream_encoder_step'


def _rmsnorm(x, g):
    xf = x.astype(jnp.float32)
    y = xf * lax.rsqrt(jnp.mean(xf * xf, axis=-1, keepdims=True) + EPS)
    return (y * g.astype(jnp.float32)).astype(x.dtype)


def _layernorm(x, g, b):
    xf = x.astype(jnp.float32)
    mu = jnp.mean(xf, axis=-1, keepdims=True)
    var = jnp.mean(jnp.square(xf - mu), axis=-1, keepdims=True)
    y = (xf - mu) * lax.rsqrt(var + EPS)
    return (y * g.astype(jnp.float32) + b.astype(jnp.float32)).astype(x.dtype)


def _causal_dwconv(x, prev, w):
    k = w.shape[0]
    xp = jnp.concatenate([prev.astype(x.dtype), x], axis=1)
    y = lax.conv_general_dilated(xp, w[:, None, :].astype(x.dtype), window_strides=(1,),
                                 padding='VALID', dimension_numbers=('NWC', 'WIO', 'NWC'),
                                 feature_group_count=x.shape[-1])
    return y, xp[:, xp.shape[1] - (k - 1):]


def _ssd(xh, dt, a_head, bm, cm, s0, chunk):
    f32 = jnp.float32
    b, l, h, p = xh.shape
    g, n = bm.shape[2], bm.shape[3]
    r = h // g
    c = l // chunk
    x = xh.astype(f32).reshape(b, c, chunk, g, r, p)
    dtc = dt.reshape(b, c, chunk, g, r)
    bmc = bm.astype(f32).reshape(b, c, chunk, g, n)
    cmc = cm.astype(f32).reshape(b, c, chunk, g, n)
    a_cum = jnp.cumsum(dtc * a_head.reshape(g, r), axis=2)
    causal = jnp.tril(jnp.ones((chunk, chunk), dtype=bool))[None, None, :, :, None, None]
    seg = a_cum[:, :, :, None] - a_cum[:, :, None, :]
    decay_qk = jnp.exp(jnp.where(causal, seg, -jnp.inf))
    cb = jnp.einsum('bcqgn,bckgn->bcqkg', cmc, bmc)
    y_diag = jnp.einsum('bcqkg,bcqkgr,bckgr,bckgrp->bcqgrp', cb, decay_qk, dtc, x)
    decay_end = jnp.exp(a_cum[:, :, -1:] - a_cum)
    chunk_states = jnp.einsum('bckgn,bckgr,bckgrp->bcgrpn', bmc, decay_end * dtc, x)
    chunk_decay = jnp.exp(a_cum[:, :, -1])

    def step(s, inp):
        dec, st = inp
        return dec[..., None, None] * s + st, s

    s_last, s_prev = lax.scan(step, s0.astype(f32).reshape(b, g, r, p, n),
                              (jnp.moveaxis(chunk_decay, 1, 0), jnp.moveaxis(chunk_states, 1, 0)))
    s_prev = jnp.moveaxis(s_prev, 0, 1)
    y_off = jnp.einsum('bcqgn,bcgrpn,bcqgr->bcqgrp', cmc, s_prev, jnp.exp(a_cum))
    y = (y_diag + y_off).reshape(b, l, h, p)
    return y, s_last.reshape(b, h, p, n).astype(s0.dtype)


def _layer(x, st_a, st_ssm, st_b, st_c, st_f, norm_mix_g, w_in, conv_a_w, conv_b_w,
           conv_b_bias, dt_bias, a_log, d_skip, ssm_norm_g, conv_c_w, conv_c_bias, ln_c_g,
           ln_c_b, w_out, norm_ffn_g, w_up, conv_ffn_w, w_down):
    f32 = jnp.float32
    b, l, _ = x.shape
    h = _rmsnorm(x, norm_mix_g)
    proj = jnp.einsum('bld,de->ble', h, w_in)
    cuts = [D_A, 2 * D_A, 3 * D_A, 3 * D_A + D_B, 3 * D_A + D_B + D_XBC,
            3 * D_A + D_B + D_XBC + SSM_HEADS]
    a_v, a_b, a_c, z, xbc, dt_raw, c_in = jnp.split(proj, cuts, axis=-1)

    u_conv, new_a = _causal_dwconv(a_c * a_v, st_a, conv_a_w)
    y_a = a_b * u_conv

    xbc_c, new_b = _causal_dwconv(xbc, st_b, conv_b_w)
    xbc_c = jax.nn.silu(xbc_c + conv_b_bias.astype(x.dtype))
    xs, bm, cm = jnp.split(xbc_c, [D_B, D_B + SSM_GROUPS * SSM_STATE], axis=-1)
    xs = xs.reshape(b, l, SSM_HEADS, SSM_HEAD_DIM)
    bm = bm.reshape(b, l, SSM_GROUPS, SSM_STATE)
    cm = cm.reshape(b, l, SSM_GROUPS, SSM_STATE)
    dt = jax.nn.softplus(dt_raw.astype(f32) + dt_bias.astype(f32))
    a_head = -jnp.exp(a_log.astype(f32))
    y_ssm, new_ssm = _ssd(xs, dt, a_head, bm, cm, st_ssm, min(SSD_CHUNK, l))
    y_ssm = y_ssm + d_skip.astype(f32)[:, None] * xs.astype(f32)
    y_b = _rmsnorm(y_ssm.reshape(b, l, D_B) * jax.nn.silu(z.astype(f32)), ssm_norm_g).astype(x.dtype)

    c_glu = c_in[..., :D_C] * jax.nn.sigmoid(c_in[..., D_C:])
    c_conv, new_c = _causal_dwconv(c_glu, st_c, conv_c_w)
    y_c = jax.nn.silu(_layernorm(c_conv + conv_c_bias.astype(x.dtype), ln_c_g, ln_c_b))

    x = x + jnp.einsum('ble,ed->bld', jnp.concatenate([y_a, y_b, y_c], axis=-1), w_out)

    h2 = _rmsnorm(x, norm_ffn_g)
    up, gate = jnp.split(jnp.einsum('bld,df->blf', h2, w_up), 2, axis=-1)
    g_conv, new_f = _causal_dwconv(gate, st_f, conv_ffn_w)
    x = x + jnp.einsum('blf,fd->bld', jax.nn.silu(g_conv) * up, w_down)
    return x, (new_a, new_ssm, new_b, new_c, new_f)


def setup_inputs(seed: int = 0) -> dict:
    key = jax.random.key(seed)
    ks = jax.random.split(key, 26)
    f32 = jnp.float32

    def nrm(k, shape, scale):
        return jax.random.normal(k, shape, f32) * scale

    def gain(k, shape):
        return 1.0 + nrm(k, shape, 0.02)

    dt0 = jnp.exp(jax.random.uniform(ks[12], (DEPTH, SSM_HEADS), f32, math.log(1e-3), math.log(1e-1)))
    return {
        'x_prompt': nrm(ks[0], (BATCH, SEQ, D_MODEL), 1.0),
        'x_sample': nrm(ks[1], (DEC_BATCH, DEC_SEQ, D_MODEL), 1.0),
        'state_conv_a': nrm(ks[2], (DEPTH, DEC_BATCH, CONV_A_W - 1, D_A), 1.0),
        'state_ssm': nrm(ks[3], (DEPTH, DEC_BATCH, SSM_HEADS, SSM_HEAD_DIM, SSM_STATE), 0.1),
        'state_conv_b': nrm(ks[4], (DEPTH, DEC_BATCH, CONV_B_W - 1, D_XBC), 1.0),
        'state_conv_c': nrm(ks[5], (DEPTH, DEC_BATCH, CONV_C_W - 1, D_C), 1.0),
        'state_conv_ffn': nrm(ks[6], (DEPTH, DEC_BATCH, CONV_F_W - 1, D_FF), 1.0),
        'norm_mix_g': gain(ks[7], (DEPTH, D_MODEL)),
        'w_in': nrm(ks[8], (DEPTH, D_MODEL, D_IN), D_MODEL ** -0.5),
        'conv_a_w': nrm(ks[9], (DEPTH, CONV_A_W, D_A), CONV_A_W ** -0.5),
        'conv_b_w': nrm(ks[10], (DEPTH, CONV_B_W, D_XBC), CONV_B_W ** -0.5),
        'conv_b_bias': nrm(ks[11], (DEPTH, D_XBC), 0.02),
        'dt_bias': dt0 + jnp.log(-jnp.expm1(-dt0)),
        'a_log': jnp.log(jax.random.uniform(ks[13], (DEPTH, SSM_HEADS), f32, 1.0, 16.0)),
        'd_skip': gain(ks[14], (DEPTH, SSM_HEADS)),
        'ssm_norm_g': gain(ks[15], (DEPTH, D_B)),
        'conv_c_w': nrm(ks[16], (DEPTH, CONV_C_W, D_C), CONV_C_W ** -0.5),
        'conv_c_bias': nrm(ks[17], (DEPTH, D_C), 0.02),
        'ln_c_g': gain(ks[18], (DEPTH, D_C)),
        'ln_c_b': nrm(ks[19], (DEPTH, D_C), 0.02),
        'w_out': nrm(ks[20], (DEPTH, D_MIX, D_MODEL), D_MIX ** -0.5),
        'norm_ffn_g': gain(ks[21], (DEPTH, D_MODEL)),
        'w_up': nrm(ks[22], (DEPTH, D_MODEL, 2 * D_FF), D_MODEL ** -0.5),
        'conv_ffn_w': nrm(ks[23], (DEPTH, CONV_F_W, D_FF), CONV_F_W ** -0.5),
        'w_down': nrm(ks[24], (DEPTH, D_FF, D_MODEL), D_FF ** -0.5),
        'final_norm_g': gain(ks[25], (D_MODEL,)),
    }


def reference(x_prompt, x_sample, state_conv_a, state_ssm, state_conv_b, state_conv_c,
              state_conv_ffn, norm_mix_g, w_in, conv_a_w, conv_b_w, conv_b_bias, dt_bias,
              a_log, d_skip, ssm_norm_g, conv_c_w, conv_c_bias, ln_c_g, ln_c_b, w_out,
              norm_ffn_g, w_up, conv_ffn_w, w_down, final_norm_g):
    bp = x_prompt.shape[0]
    dtp = x_prompt.dtype
    zero_states = (jnp.zeros((bp, CONV_A_W - 1, D_A), dtp),
                   jnp.zeros((bp, SSM_HEADS, SSM_HEAD_DIM, SSM_STATE), dtp),
                   jnp.zeros((bp, CONV_B_W - 1, D_XBC), dtp),
                   jnp.zeros((bp, CONV_C_W - 1, D_C), dtp),
                   jnp.zeros((bp, CONV_F_W - 1, D_FF), dtp))
    hp, hs = x_prompt, x_sample
    p_a, p_ssm, p_b, p_c, p_f = [], [], [], [], []
    s_a, s_ssm, s_b, s_c, s_f = [], [], [], [], []
    for i in range(DEPTH):
        params = (norm_mix_g[i], w_in[i], conv_a_w[i], conv_b_w[i], conv_b_bias[i], dt_bias[i],
                  a_log[i], d_skip[i], ssm_norm_g[i], conv_c_w[i], conv_c_bias[i], ln_c_g[i],
                  ln_c_b[i], w_out[i], norm_ffn_g[i], w_up[i], conv_ffn_w[i], w_down[i])
        hp, (na, nssm, nb, nc, nf) = _layer(hp, *zero_states, *params)
        p_a.append(na); p_ssm.append(nssm); p_b.append(nb); p_c.append(nc); p_f.append(nf)
        hs, (na, nssm, nb, nc, nf) = _layer(hs, state_conv_a[i], state_ssm[i], state_conv_b[i],
                                            state_conv_c[i], state_conv_ffn[i], *params)
        s_a.append(na); s_ssm.append(nssm); s_b.append(nb); s_c.append(nc); s_f.append(nf)
    y_prompt = _rmsnorm(hp, final_norm_g)
    y_sample = _rmsnorm(hs, final_norm_g)
    return (y_prompt, y_sample,
            jnp.stack(p_a), jnp.stack(p_ssm), jnp.stack(p_b), jnp.stack(p_c), jnp.stack(p_f),
            jnp.stack(s_a), jnp.stack(s_ssm), jnp.stack(s_b), jnp.stack(s_c), jnp.stack(s_f))
```

```python
import functools

import jax
import jax.numpy as jnp
from jax import lax
from jax.experimental import pallas as pl
from jax.experimental.pallas import tpu as pltpu

F32 = jnp.float32
BF16 = jnp.bfloat16

D_MODEL = 1024
D_A = 512
D_B = 1024
D_C = 512
N_HEADS = 16
HEAD_DIM = 64
N_GROUPS = 2
HEADS_PER_GROUP = N_HEADS // N_GROUPS
D_STATE = 128
D_XBC = D_B + 2 * N_GROUPS * D_STATE
D_FF = 2816
D_MIX = D_A + D_B + D_C
CONV_A_W, CONV_B_W, CONV_C_W, CONV_F_W = 3, 4, 31, 3
EPS = 1e-5
SSD_CHUNK = 64

LANES = 128
SUBLANES = 8
HEAD_PAD = LANES
VMEM_LIMIT_BYTES = 56 * 1024 * 1024


def _dot(a, b):
    return jnp.dot(a, b, preferred_element_type=F32)


def _dot_nt(a, b):
    return lax.dot_general(a, b, (((1,), (1,)), ((), ())), preferred_element_type=F32)


def _split3(x):
    hi = x.astype(BF16)
    r1 = x - hi.astype(F32)
    mid = r1.astype(BF16)
    r2 = r1 - mid.astype(F32)
    return hi, mid, r2.astype(BF16)


def _silu(x):
    return x * jax.nn.sigmoid(x)


def _rms_scale(x):
    return x * lax.rsqrt(jnp.mean(x * x, axis=-1, keepdims=True) + EPS)


def _dwconv_blocks(buf_ref, w_ref, k, pad, n_b, n_t, rows, emit):
    for b in range(n_b):
        for r in range(0, n_t, rows):
            acc = None
            for j in range(k):
                s = pad - (k - 1) + j + r
                term = buf_ref[b, s:s + rows, :] * w_ref[j:j + 1, :]
                acc = term if acc is None else acc + term
            emit(b, r, acc)


def _carry_tail(buf_ref, out_ref, k, pad, n_t):
    tail = buf_ref[:, pad + n_t - (k - 1):pad + n_t, :]
    buf_ref[:, pad - (k - 1):pad, :] = tail
    out_ref[...] = tail


PAD_A, PAD_B, PAD_C, PAD_F = 8, 8, 32, 8


def _mixer_kernel(x_ref, sta_ref, sts_ref, stb_ref, stc_ref,
                  gmix_ref, wa_ref, wz_ref, wxbc_ref, wc_ref, wdt_ref,
                  cwa_ref, cwb_ref, cbb_ref, dtb_ref, alog_ref, dskip_ref, gssm_ref,
                  cwc_ref, cbc_ref, lng_ref, lnb_ref, wout_ref, e3p_ref, e3q_ref,
                  xo_ref, na_ref, ns_ref, nb_ref, nc_ref,
                  bufa, bufb, bufc, st_scr, xbc_scr, dt_scr, ycat_scr,
                  *, n_b, n_t, q):
    t = pl.program_id(1)
    m = n_b * n_t
    hq = N_HEADS * q

    @pl.when(t == 0)
    def _load_state():
        bufa[:, PAD_A - (CONV_A_W - 1):PAD_A, :] = sta_ref[...]
        bufb[:, PAD_B - (CONV_B_W - 1):PAD_B, :] = stb_ref[...]
        bufc[:, PAD_C - (CONV_C_W - 1):PAD_C, :] = stc_ref[...]
        for b in range(n_b):
            st_scr[b] = sts_ref[b].T

    x = x_ref[...].reshape(m, D_MODEL)
    h = (_rms_scale(x) * gmix_ref[...]).astype(BF16)

    pa = _dot(h, wa_ref[...])
    a_b = pa[:, D_A:2 * D_A]
    bufa[:, PAD_A:PAD_A + n_t, :] = (pa[:, 2 * D_A:] * pa[:, :D_A]).reshape(n_b, n_t, D_A)

    def emit_a(b, r, y):
        r0 = b * n_t + r
        ycat_scr[r0:r0 + y.shape[0], 0:D_A] = (a_b[r0:r0 + y.shape[0]] * y).astype(BF16)

    _dwconv_blocks(bufa, cwa_ref, CONV_A_W, PAD_A, n_b, n_t, min(n_t, 32), emit_a)
    _carry_tail(bufa, na_ref, CONV_A_W, PAD_A, n_t)

    pc = _dot(h, wc_ref[...])
    bufc[:, PAD_C:PAD_C + n_t, :] = (pc[:, :D_C] * jax.nn.sigmoid(pc[:, D_C:])).reshape(n_b, n_t, D_C)

    def emit_c(b, r, y):
        r0 = b * n_t + r
        y = y + cbc_ref[...]
        mu = jnp.mean(y, axis=-1, keepdims=True)
        yc = y - mu
        var = jnp.mean(yc * yc, axis=-1, keepdims=True)
        yn = yc * lax.rsqrt(var + EPS) * lng_ref[...] + lnb_ref[...]
        ycat_scr[r0:r0 + y.shape[0], D_A + D_B:D_MIX] = _silu(yn).astype(BF16)

    _dwconv_blocks(bufc, cwc_ref, CONV_C_W, PAD_C, n_b, n_t, min(n_t, 32), emit_c)
    _carry_tail(bufc, nc_ref, CONV_C_W, PAD_C, n_t)

    bufb[:, PAD_B:PAD_B + n_t, :] = _dot(h, wxbc_ref[...]).reshape(n_b, n_t, D_XBC)

    def emit_b(b, r, y):
        r0 = b * n_t + r
        xbc_scr[r0:r0 + y.shape[0], :] = _silu(y + cbb_ref[...])

    _dwconv_blocks(bufb, cwb_ref, CONV_B_W, PAD_B, n_b, n_t, min(n_t, 16), emit_b)
    _carry_tail(bufb, nb_ref, CONV_B_W, PAD_B, n_t)

    dt_scr[...] = jax.nn.softplus(_dot(h, wdt_ref[...]) + dtb_ref[...])
    pz = _dot(h, wz_ref[...])

    a_row = -jnp.exp(alog_ref[...])
    row = lax.broadcasted_iota(jnp.int32, (q, q), 0)
    col = lax.broadcasted_iota(jnp.int32, (q, q), 1)
    tril = (col <= row).astype(BF16)
    ones3 = jnp.ones((q, 3 * q), BF16)
    rowq = lax.broadcasted_iota(jnp.int32, (q, hq), 0)
    colq = lax.broadcasted_iota(jnp.int32, (q, hq), 1) % q
    causal_rep = colq <= rowq
    eye_rep = (colq == rowq).astype(F32)
    heads_per_blk = LANES // q
    blk_w = heads_per_blk * HEAD_DIM
    bd_r = lax.broadcasted_iota(jnp.int32, (LANES, blk_w), 0) // q
    bd_c = lax.broadcasted_iota(jnp.int32, (LANES, blk_w), 1) // HEAD_DIM
    blockdiag = bd_r == bd_c

    for b in range(n_b):
        for c in range(n_t // q):
            r0 = b * n_t + c * q
            dt_c = dt_scr[r0:r0 + q, :]
            acs = _dot(tril, jnp.concatenate(_split3(dt_c * a_row), axis=-1))
            a_cum = acs[:, :HEAD_PAD] + acs[:, HEAD_PAD:2 * HEAD_PAD] + acs[:, 2 * HEAD_PAD:]
            lhs = jnp.concatenate([jnp.concatenate(_split3(a_cum), axis=-1),
                                   jnp.concatenate(_split3(dt_c), axis=-1)], axis=0)
            ex = _dot(lhs, e3p_ref[...])
            a_x, dt_x = ex[:q], ex[q:]
            if q == HEAD_DIM:
                a_q = a_x
            else:
                a_q = _dot(lhs[:q], e3q_ref[...])
            a_row_b = _dot(ones3, jnp.concatenate(_split3(a_q * eye_rep), axis=0))
            decay = jnp.exp(jnp.where(causal_rep, a_q - a_row_b, -jnp.inf))

            xbc_c = xbc_scr[r0:r0 + q, :]
            xs = xbc_c[:, :D_B]
            bm = xbc_c[:, D_B:D_B + N_GROUPS * D_STATE]
            cm = xbc_c[:, D_B + N_GROUPS * D_STATE:]
            cm_b = cm.astype(BF16)
            bm_b = bm.astype(BF16)
            cbt = jnp.concatenate(
                [_dot_nt(cm_b[:, g * D_STATE:(g + 1) * D_STATE],
                         jnp.concatenate([bm_b[:, g * D_STATE:(g + 1) * D_STATE]] * HEADS_PER_GROUP, axis=0))
                 for g in range(N_GROUPS)], axis=-1)
            m_all = (cbt * decay).astype(BF16)
            xdt = xs * dt_x
            xdt_b = xdt.astype(BF16)
            y_parts = []
            for blk in range(N_HEADS // heads_per_blk):
                xb = xdt_b[:, blk * blk_w:(blk + 1) * blk_w]
                rhs = jnp.where(blockdiag, jnp.concatenate([xb] * heads_per_blk, axis=0), 0)
                y_parts.append(_dot(m_all[:, blk * LANES:(blk + 1) * LANES], rhs))
            y_diag = jnp.concatenate(y_parts, axis=-1)

            st = st_scr[b]
            st_b = st.astype(BF16)
            gw = HEADS_PER_GROUP * HEAD_DIM
            y_off = jnp.concatenate(
                [_dot(cm_b[:, g * D_STATE:(g + 1) * D_STATE], st_b[:, g * gw:(g + 1) * gw])
                 for g in range(N_GROUPS)], axis=-1)
            y = y_diag + y_off * jnp.exp(a_x) + dskip_ref[...] * xs

            a_last = a_x[q - 1:q, :]
            xw_b = (xdt * jnp.exp(a_last - a_x)).astype(BF16)
            upd = jnp.concatenate(
                [_dot(bm[:, g * D_STATE:(g + 1) * D_STATE].T.astype(BF16), xw_b[:, g * gw:(g + 1) * gw])
                 for g in range(N_GROUPS)], axis=-1)
            st_scr[b] = st * jnp.exp(a_last) + upd

            yg = y * _silu(pz[r0:r0 + q])
            ycat_scr[r0:r0 + q, D_A:D_A + D_B] = (_rms_scale(yg) * gssm_ref[...]).astype(BF16)

    @pl.when(t == pl.num_programs(1) - 1)
    def _store_state():
        for b in range(n_b):
            ns_ref[b] = st_scr[b].T

    out = x + _dot(ycat_scr[...], wout_ref[...])
    xo_ref[...] = out.reshape(n_b, n_t, D_MODEL)


def _expand_matrix(width):
    head = jnp.arange(HEAD_PAD)[:, None]
    lane_head = (jnp.arange(N_HEADS * width) // width)[None, :]
    e = (head == lane_head).astype(BF16)
    return jnp.concatenate([e, e, e], axis=0)


def _const_spec(shape):
    nd = len(shape)
    return pl.BlockSpec(shape, lambda b, t: (0,) * nd, pipeline_mode=pl.Buffered(1))


def _mixer_call(x, st_a, st_ssm, st_b, st_c, w, *, n_b, n_t):
    bsz, seq, _ = x.shape
    q = min(SSD_CHUNK, n_t)
    assert bsz % n_b == 0 and seq % n_t == 0 and n_t % q == 0 and LANES % q == 0
    assert n_t >= CONV_C_W - 1 and n_t % SUBLANES == 0
    m = n_b * n_t
    grid = (bsz // n_b, seq // n_t)

    def bspec(shape):
        return pl.BlockSpec((n_b,) + shape, lambda b, t: (b, 0, 0))

    consts = [w['gmix'], w['wa'], w['wz'], w['wxbc'], w['wc'], w['wdt'],
              w['cwa'], w['cwb'], w['cbb'], w['dtb'], w['alog'], w['dskip'], w['gssm'],
              w['cwc'], w['cbc'], w['lng'], w['lnb'], w['wout'],
              _expand_matrix(HEAD_DIM), _expand_matrix(q)]
    in_specs = ([pl.BlockSpec((n_b, n_t, D_MODEL), lambda b, t: (b, t, 0)),
                 bspec((CONV_A_W - 1, D_A)), bspec((D_B, D_STATE)),
                 bspec((CONV_B_W - 1, D_XBC)), bspec((CONV_C_W - 1, D_C))]
                + [_const_spec(c.shape) for c in consts])
    out_shape = (jax.ShapeDtypeStruct(x.shape, F32),
                 jax.ShapeDtypeStruct(st_a.shape, F32),
                 jax.ShapeDtypeStruct(st_ssm.shape, F32),
                 jax.ShapeDtypeStruct(st_b.shape, F32),
                 jax.ShapeDtypeStruct(st_c.shape, F32))
    out_specs = (pl.BlockSpec((n_b, n_t, D_MODEL), lambda b, t: (b, t, 0)),
                 bspec((CONV_A_W - 1, D_A)), bspec((D_B, D_STATE)),
                 bspec((CONV_B_W - 1, D_XBC)), bspec((CONV_C_W - 1, D_C)))
    scratch = [pltpu.VMEM((n_b, PAD_A + n_t, D_A), F32),
               pltpu.VMEM((n_b, PAD_B + n_t, D_XBC), F32),
               pltpu.VMEM((n_b, PAD_C + n_t, D_C), F32),
               pltpu.VMEM((n_b, D_STATE, D_B), F32),
               pltpu.VMEM((m, D_XBC), F32),
               pltpu.VMEM((m, HEAD_PAD), F32),
               pltpu.VMEM((m, D_MIX), BF16)]
    return pl.pallas_call(
        functools.partial(_mixer_kernel, n_b=n_b, n_t=n_t, q=q),
        grid=grid, in_specs=in_specs, out_specs=out_specs, out_shape=out_shape,
        scratch_shapes=scratch,
        compiler_params=pltpu.CompilerParams(
            dimension_semantics=("arbitrary", "arbitrary"),
            vmem_limit_bytes=VMEM_LIMIT_BYTES),
        name="mixer",
    )(x, st_a, st_ssm, st_b, st_c, *consts)


def _ffn_kernel(x_ref, stf_ref, gffn_ref, wu_ref, wg_ref, cwf_ref, wd_ref, gfin_ref,
                xo_ref, nf_ref, buff, hid_scr, *, n_b, n_t, final_norm):
    t = pl.program_id(1)
    m = n_b * n_t

    @pl.when(t == 0)
    def _load_state():
        buff[:, PAD_F - (CONV_F_W - 1):PAD_F, :] = stf_ref[...]

    x = x_ref[...].reshape(m, D_MODEL)
    h = (_rms_scale(x) * gffn_ref[...]).astype(BF16)
    up = _dot(h, wu_ref[...])
    buff[:, PAD_F:PAD_F + n_t, :] = _dot(h, wg_ref[...]).reshape(n_b, n_t, D_FF)

    def emit_f(b, r, y):
        r0 = b * n_t + r
        hid_scr[r0:r0 + y.shape[0], :] = (_silu(y) * up[r0:r0 + y.shape[0]]).astype(BF16)

    _dwconv_blocks(buff, cwf_ref, CONV_F_W, PAD_F, n_b, n_t, SUBLANES, emit_f)
    _carry_tail(buff, nf_ref, CONV_F_W, PAD_F, n_t)

    out = x + _dot(hid_scr[...], wd_ref[...])
    if final_norm:
        out = _rms_scale(out) * gfin_ref[...]
    xo_ref[...] = out.reshape(n_b, n_t, D_MODEL)


def _ffn_call(x, st_f, w, gfin, *, n_b, n_t, final_norm):
    bsz, seq, _ = x.shape
    assert bsz % n_b == 0 and seq % n_t == 0 and n_t % SUBLANES == 0
    m = n_b * n_t
    grid = (bsz // n_b, seq // n_t)
    consts = [w['gffn'], w['wu'], w['wg'], w['cwf'], w['wd'], gfin]
    in_specs = ([pl.BlockSpec((n_b, n_t, D_MODEL), lambda b, t: (b, t, 0)),
                 pl.BlockSpec((n_b, CONV_F_W - 1, D_FF), lambda b, t: (b, 0, 0))]
                + [_const_spec(c.shape) for c in consts])
    out_shape = (jax.ShapeDtypeStruct(x.shape, F32), jax.ShapeDtypeStruct(st_f.shape, F32))
    out_specs = (pl.BlockSpec((n_b, n_t, D_MODEL), lambda b, t: (b, t, 0)),
                 pl.BlockSpec((n_b, CONV_F_W - 1, D_FF), lambda b, t: (b, 0, 0)))
    scratch = [pltpu.VMEM((n_b, PAD_F + n_t, D_FF), F32),
               pltpu.VMEM((m, D_FF), BF16)]
    return pl.pallas_call(
        functools.partial(_ffn_kernel, n_b=n_b, n_t=n_t, final_norm=final_norm),
        grid=grid, in_specs=in_specs, out_specs=out_specs, out_shape=out_shape,
        scratch_shapes=scratch,
        compiler_params=pltpu.CompilerParams(
            dimension_semantics=("arbitrary", "arbitrary"),
            vmem_limit_bytes=VMEM_LIMIT_BYTES),
        name="ffn",
    )(x, st_f, *consts)


def _row(v, pad_to=None):
    v = v.astype(F32).reshape(1, -1)
    if pad_to is not None:
        v = jnp.pad(v, ((0, 0), (0, pad_to - v.shape[1])))
    return v


def _layer_weights(i, norm_mix_g, w_in, conv_a_w, conv_b_w, conv_b_bias, dt_bias, a_log, d_skip,
                   ssm_norm_g, conv_c_w, conv_c_bias, ln_c_g, ln_c_b, w_out, norm_ffn_g, w_up,
                   conv_ffn_w, w_down):
    wi = w_in[i]
    c0 = 3 * D_A
    c1 = c0 + D_B
    c2 = c1 + D_XBC
    c3 = c2 + N_HEADS
    return dict(
        gmix=_row(norm_mix_g[i]),
        wa=wi[:, :c0].astype(BF16), wz=wi[:, c0:c1].astype(BF16),
        wxbc=wi[:, c1:c2].astype(BF16),
        wdt=jnp.pad(wi[:, c2:c3], ((0, 0), (0, HEAD_PAD - N_HEADS))).astype(BF16),
        wc=wi[:, c3:].astype(BF16),
        cwa=conv_a_w[i], cwb=conv_b_w[i], cbb=_row(conv_b_bias[i]),
        dtb=_row(dt_bias[i], HEAD_PAD), alog=_row(a_log[i], HEAD_PAD),
        dskip=_row(jnp.repeat(d_skip[i], HEAD_DIM)), gssm=_row(ssm_norm_g[i]),
        cwc=conv_c_w[i], cbc=_row(conv_c_bias[i]), lng=_row(ln_c_g[i]), lnb=_row(ln_c_b[i]),
        wout=w_out[i].astype(BF16),
        gffn=_row(norm_ffn_g[i]),
        wu=w_up[i][:, :D_FF].astype(BF16), wg=w_up[i][:, D_FF:].astype(BF16),
        cwf=conv_ffn_w[i], wd=w_down[i].astype(BF16),
    )


def _layer(x, st_a, st_ssm, st_b, st_c, st_f, w, gfin, *, n_b, n_t, final_norm):
    bsz = x.shape[0]
    x, na, ns, nb, nc = _mixer_call(x, st_a, st_ssm.reshape(bsz, D_B, D_STATE), st_b, st_c, w,
                                    n_b=n_b, n_t=n_t)
    x, nf = _ffn_call(x, st_f, w, gfin, n_b=n_b, n_t=n_t, final_norm=final_norm)
    return x, (na, ns.reshape(bsz, N_HEADS, HEAD_DIM, D_STATE), nb, nc, nf)


PROMPT_TILE = (1, 256)
SAMPLE_TILE = (8, 32)


def kernel(x_prompt, x_sample, state_conv_a, state_ssm, state_conv_b, state_conv_c, state_conv_ffn, norm_mix_g, w_in, conv_a_w, conv_b_w, conv_b_bias, dt_bias, a_log, d_skip, ssm_norm_g, conv_c_w, conv_c_bias, ln_c_g, ln_c_b, w_out, norm_ffn_g, w_up, conv_ffn_w, w_down, final_norm_g):
    depth = w_in.shape[0]
    bp = x_prompt.shape[0]
    zeros = (jnp.zeros((bp, CONV_A_W - 1, D_A), F32),
             jnp.zeros((bp, N_HEADS, HEAD_DIM, D_STATE), F32),
             jnp.zeros((bp, CONV_B_W - 1, D_XBC), F32),
             jnp.zeros((bp, CONV_C_W - 1, D_C), F32),
             jnp.zeros((bp, CONV_F_W - 1, D_FF), F32))
    gfin = _row(final_norm_g)
    hp, hs = x_prompt, x_sample
    p_states, s_states = [], []
    for i in range(depth):
        w = _layer_weights(i, norm_mix_g, w_in, conv_a_w, conv_b_w, conv_b_bias, dt_bias, a_log,
                           d_skip, ssm_norm_g, conv_c_w, conv_c_bias, ln_c_g, ln_c_b, w_out,
                           norm_ffn_g, w_up, conv_ffn_w, w_down)
        last = i == depth - 1
        hp, st = _layer(hp, *zeros, w, gfin, n_b=PROMPT_TILE[0], n_t=PROMPT_TILE[1], final_norm=last)
        p_states.append(st)
        hs, st = _layer(hs, state_conv_a[i], state_ssm[i], state_conv_b[i], state_conv_c[i],
                        state_conv_ffn[i], w, gfin, n_b=SAMPLE_TILE[0], n_t=SAMPLE_TILE[1],
                        final_norm=last)
        s_states.append(st)
    p_out = [jnp.stack([s[j] for s in p_states]) for j in range(5)]
    s_out = [jnp.stack([s[j] for s in s_states]) for j in range(5)]
    return (hp, hs, *p_out, *s_out)
```

```python
import functools
from typing import NamedTuple

import jax
import jax.numpy as jnp
from jax import lax
from jax.experimental import pallas as pl
from jax.experimental.pallas import tpu as pltpu

F32 = jnp.float32
BF16 = jnp.bfloat16

D_MODEL = 1024
D_A = 512
D_B = 1024
D_C = 512
N_HEADS = 16
HEAD_DIM = 64
N_GROUPS = 2
HEADS_PER_GROUP = N_HEADS // N_GROUPS
D_STATE = 128
D_XBC = D_B + 2 * N_GROUPS * D_STATE
D_FF = 2816
D_MIX = D_A + D_B + D_C
CONV_A_W, CONV_B_W, CONV_C_W, CONV_F_W = 3, 4, 31, 3
EPS = 1e-5
SSD_CHUNK = 64

LANES = 128
SUBLANES = 8
HEAD_PAD = LANES
VMEM_LIMIT_BYTES = 56 * 1024 * 1024


def _dot(a, b):
    return jnp.dot(a, b, preferred_element_type=F32)


def _dot_nt(a, b):
    return lax.dot_general(a, b, (((1,), (1,)), ((), ())), preferred_element_type=F32)


def _split3(x):
    hi = x.astype(BF16)
    r1 = x - hi.astype(F32)
    mid = r1.astype(BF16)
    r2 = r1 - mid.astype(F32)
    return hi, mid, r2.astype(BF16)


def _silu(x):
    return x * jax.nn.sigmoid(x)


def _rms_scale(x):
    return x * lax.rsqrt(jnp.mean(x * x, axis=-1, keepdims=True) + EPS)


def _n_shifts(k):
    return min(k, SUBLANES)


def _conv_buf(k, pad, n_b, n_t, c):
    return pltpu.VMEM((_n_shifts(k), n_b, pad + n_t + SUBLANES, c), F32)


def _conv_load_tail(buf_ref, st_ref, k, pad):
    for v in range(_n_shifts(k)):
        buf_ref[v, :, pad - (k - 1) + v:pad + v, :] = st_ref[...]


def _conv_stage(buf_ref, val, k, pad, n_t):
    for v in range(_n_shifts(k)):
        buf_ref[v, :, pad + v:pad + v + n_t, :] = val


def _dwconv_blocks(buf_ref, w_ref, k, pad, n_b, n_t, rows, emit):
    for b in range(n_b):
        for r in range(0, n_t, rows):
            acc = None
            for j in range(k):
                s = pad - (k - 1) + j + r
                v = (-s) % SUBLANES
                term = buf_ref[v, b, s + v:s + v + rows, :] * w_ref[j:j + 1, :]
                acc = term if acc is None else acc + term
            emit(b, r, acc)


def _carry_tail(buf_ref, out_ref, k, pad, n_t):
    out_ref[...] = buf_ref[0, :, pad + n_t - (k - 1):pad + n_t, :]
    for v in range(_n_shifts(k)):
        buf_ref[v, :, pad - (k - 1) + v:pad + v, :] = buf_ref[v, :, pad + n_t - (k - 1) + v:pad + n_t + v, :]


PAD_A, PAD_B, PAD_C, PAD_F = 8, 8, 32, 8


def _mixer_kernel(x_ref, sta_ref, sts_ref, stb_ref, stc_ref,
                  gmix_ref, wa_ref, wz_ref, wxbc_ref, wc_ref, wdt_ref,
                  cwa_ref, cwb_ref, cbb_ref, dtb_ref, alog_ref, dskip_ref, gssm_ref,
                  cwc_ref, cbc_ref, lng_ref, lnb_ref, wout_ref, e3p_ref, e3q_ref,
                  xo_ref, na_ref, ns_ref, nb_ref, nc_ref,
                  bufa, bufb, bufc, st_scr, xbc_scr, dt_scr, ycat_scr,
                  *, n_b, n_t, q):
    t = pl.program_id(1)
    m = n_b * n_t
    hq = N_HEADS * q

    @pl.when(t == 0)
    def _load_state():
        _conv_load_tail(bufa, sta_ref, CONV_A_W, PAD_A)
        _conv_load_tail(bufb, stb_ref, CONV_B_W, PAD_B)
        _conv_load_tail(bufc, stc_ref, CONV_C_W, PAD_C)
        for b in range(n_b):
            st_scr[b] = sts_ref[b].T

    x = x_ref[...].reshape(m, D_MODEL)
    h = (_rms_scale(x) * gmix_ref[...]).astype(BF16)

    pa = _dot(h, wa_ref[...])
    a_b = pa[:, D_A:2 * D_A]
    _conv_stage(bufa, (pa[:, 2 * D_A:] * pa[:, :D_A]).reshape(n_b, n_t, D_A), CONV_A_W, PAD_A, n_t)

    def emit_a(b, r, y):
        r0 = b * n_t + r
        ycat_scr[r0:r0 + y.shape[0], 0:D_A] = (a_b[r0:r0 + y.shape[0]] * y).astype(BF16)

    _dwconv_blocks(bufa, cwa_ref, CONV_A_W, PAD_A, n_b, n_t, min(n_t, 32), emit_a)
    _carry_tail(bufa, na_ref, CONV_A_W, PAD_A, n_t)

    pc = _dot(h, wc_ref[...])
    _conv_stage(bufc, (pc[:, :D_C] * jax.nn.sigmoid(pc[:, D_C:])).reshape(n_b, n_t, D_C),
                CONV_C_W, PAD_C, n_t)

    def emit_c(b, r, y):
        r0 = b * n_t + r
        y = y + cbc_ref[...]
        mu = jnp.mean(y, axis=-1, keepdims=True)
        yc = y - mu
        var = jnp.mean(yc * yc, axis=-1, keepdims=True)
        yn = yc * lax.rsqrt(var + EPS) * lng_ref[...] + lnb_ref[...]
        ycat_scr[r0:r0 + y.shape[0], D_A + D_B:D_MIX] = _silu(yn).astype(BF16)

    _dwconv_blocks(bufc, cwc_ref, CONV_C_W, PAD_C, n_b, n_t, min(n_t, 32), emit_c)
    _carry_tail(bufc, nc_ref, CONV_C_W, PAD_C, n_t)

    _conv_stage(bufb, _dot(h, wxbc_ref[...]).reshape(n_b, n_t, D_XBC), CONV_B_W, PAD_B, n_t)

    def emit_b(b, r, y):
        r0 = b * n_t + r
        xbc_scr[r0:r0 + y.shape[0], :] = _silu(y + cbb_ref[...])

    _dwconv_blocks(bufb, cwb_ref, CONV_B_W, PAD_B, n_b, n_t, min(n_t, 16), emit_b)
    _carry_tail(bufb, nb_ref, CONV_B_W, PAD_B, n_t)

    dt_scr[...] = jax.nn.softplus(_dot(h, wdt_ref[...]) + dtb_ref[...])
    pz = _dot(h, wz_ref[...])

    a_row = -jnp.exp(alog_ref[...])
    row = lax.broadcasted_iota(jnp.int32, (q, q), 0)
    col = lax.broadcasted_iota(jnp.int32, (q, q), 1)
    tril = (col <= row).astype(BF16)
    ones3 = jnp.ones((q, 3 * q), BF16)
    rowq = lax.broadcasted_iota(jnp.int32, (q, hq), 0)
    colq = lax.broadcasted_iota(jnp.int32, (q, hq), 1) % q
    causal_rep = colq <= rowq
    eye_rep = (colq == rowq).astype(F32)
    heads_per_blk = LANES // q
    blk_w = heads_per_blk * HEAD_DIM
    bd_r = lax.broadcasted_iota(jnp.int32, (LANES, blk_w), 0) // q
    bd_c = lax.broadcasted_iota(jnp.int32, (LANES, blk_w), 1) // HEAD_DIM
    blockdiag = bd_r == bd_c

    for b in range(n_b):
        for c in range(n_t // q):
            r0 = b * n_t + c * q
            dt_c = dt_scr[r0:r0 + q, :]
            acs = _dot(tril, jnp.concatenate(_split3(dt_c * a_row), axis=-1))
            a_cum = acs[:, :HEAD_PAD] + acs[:, HEAD_PAD:2 * HEAD_PAD] + acs[:, 2 * HEAD_PAD:]
            lhs = jnp.concatenate([jnp.concatenate(_split3(a_cum), axis=-1),
                                   jnp.concatenate(_split3(dt_c), axis=-1)], axis=0)
            ex = _dot(lhs, e3p_ref[...])
            a_x, dt_x = ex[:q], ex[q:]
            if q == HEAD_DIM:
                a_q = a_x
            else:
                a_q = _dot(lhs[:q], e3q_ref[...])
            a_row_b = _dot(ones3, jnp.concatenate(_split3(a_q * eye_rep), axis=0))
            decay = jnp.exp(jnp.where(causal_rep, a_q - a_row_b, -jnp.inf))

            xbc_c = xbc_scr[r0:r0 + q, :]
            xs = xbc_c[:, :D_B]
            bm = xbc_c[:, D_B:D_B + N_GROUPS * D_STATE]
            cm = xbc_c[:, D_B + N_GROUPS * D_STATE:]
            cm_b = cm.astype(BF16)
            bm_b = bm.astype(BF16)
            cbt = jnp.concatenate(
                [_dot_nt(cm_b[:, g * D_STATE:(g + 1) * D_STATE],
                         jnp.concatenate([bm_b[:, g * D_STATE:(g + 1) * D_STATE]] * HEADS_PER_GROUP, axis=0))
                 for g in range(N_GROUPS)], axis=-1)
            m_all = (cbt * decay).astype(BF16)
            xdt = xs * dt_x
            xdt_b = xdt.astype(BF16)
            y_parts = []
            for blk in range(N_HEADS // heads_per_blk):
                xb = xdt_b[:, blk * blk_w:(blk + 1) * blk_w]
                rhs = jnp.where(blockdiag, jnp.concatenate([xb] * heads_per_blk, axis=0), 0)
                y_parts.append(_dot(m_all[:, blk * LANES:(blk + 1) * LANES], rhs))
            y_diag = jnp.concatenate(y_parts, axis=-1)

            st = st_scr[b]
            st_b = st.astype(BF16)
            gw = HEADS_PER_GROUP * HEAD_DIM
            y_off = jnp.concatenate(
                [_dot(cm_b[:, g * D_STATE:(g + 1) * D_STATE], st_b[:, g * gw:(g + 1) * gw])
                 for g in range(N_GROUPS)], axis=-1)
            y = y_diag + y_off * jnp.exp(a_x) + dskip_ref[...] * xs

            a_last = a_x[q - 1:q, :]
            xw_b = (xdt * jnp.exp(a_last - a_x)).astype(BF16)
            upd = jnp.concatenate(
                [_dot(bm[:, g * D_STATE:(g + 1) * D_STATE].T.astype(BF16), xw_b[:, g * gw:(g + 1) * gw])
                 for g in range(N_GROUPS)], axis=-1)
            st_scr[b] = st * jnp.exp(a_last) + upd

            yg = y * _silu(pz[r0:r0 + q])
            ycat_scr[r0:r0 + q, D_A:D_A + D_B] = (_rms_scale(yg) * gssm_ref[...]).astype(BF16)

    @pl.when(t == pl.num_programs(1) - 1)
    def _store_state():
        for b in range(n_b):
            ns_ref[b] = st_scr[b].T

    out = x + _dot(ycat_scr[...], wout_ref[...])
    xo_ref[...] = out.reshape(n_b, n_t, D_MODEL)


def _expand_matrix(width):
    head = jnp.arange(HEAD_PAD)[:, None]
    lane_head = (jnp.arange(N_HEADS * width) // width)[None, :]
    e = (head == lane_head).astype(BF16)
    return jnp.concatenate([e, e, e], axis=0)


def _const_spec(shape):
    nd = len(shape)
    return pl.BlockSpec(shape, lambda b, t: (0,) * nd, pipeline_mode=pl.Buffered(1))


def _mixer_call(x, st_a, st_ssm, st_b, st_c, w, *, n_b, n_t):
    bsz, seq, _ = x.shape
    q = min(SSD_CHUNK, n_t)
    assert bsz % n_b == 0 and seq % n_t == 0 and n_t % q == 0 and LANES % q == 0
    assert n_t >= CONV_C_W - 1 and n_t % SUBLANES == 0
    m = n_b * n_t
    grid = (bsz // n_b, seq // n_t)

    def bspec(shape):
        return pl.BlockSpec((n_b,) + shape, lambda b, t: (b, 0, 0))

    consts = [w['gmix'], w['wa'], w['wz'], w['wxbc'], w['wc'], w['wdt'],
              w['cwa'], w['cwb'], w['cbb'], w['dtb'], w['alog'], w['dskip'], w['gssm'],
              w['cwc'], w['cbc'], w['lng'], w['lnb'], w['wout'],
              _expand_matrix(HEAD_DIM), _expand_matrix(q)]
    in_specs = ([pl.BlockSpec((n_b, n_t, D_MODEL), lambda b, t: (b, t, 0)),
                 bspec((CONV_A_W - 1, D_A)), bspec((D_B, D_STATE)),
                 bspec((CONV_B_W - 1, D_XBC)), bspec((CONV_C_W - 1, D_C))]
                + [_const_spec(c.shape) for c in consts])
    out_shape = (jax.ShapeDtypeStruct(x.shape, F32),
                 jax.ShapeDtypeStruct(st_a.shape, F32),
                 jax.ShapeDtypeStruct(st_ssm.shape, F32),
                 jax.ShapeDtypeStruct(st_b.shape, F32),
                 jax.ShapeDtypeStruct(st_c.shape, F32))
    out_specs = (pl.BlockSpec((n_b, n_t, D_MODEL), lambda b, t: (b, t, 0)),
                 bspec((CONV_A_W - 1, D_A)), bspec((D_B, D_STATE)),
                 bspec((CONV_B_W - 1, D_XBC)), bspec((CONV_C_W - 1, D_C)))
    scratch = [_conv_buf(CONV_A_W, PAD_A, n_b, n_t, D_A),
               _conv_buf(CONV_B_W, PAD_B, n_b, n_t, D_XBC),
               _conv_buf(CONV_C_W, PAD_C, n_b, n_t, D_C),
               pltpu.VMEM((n_b, D_STATE, D_B), F32),
               pltpu.VMEM((m, D_XBC), F32),
               pltpu.VMEM((m, HEAD_PAD), F32),
               pltpu.VMEM((m, D_MIX), BF16)]
    return pl.pallas_call(
        functools.partial(_mixer_kernel, n_b=n_b, n_t=n_t, q=q),
        grid=grid, in_specs=in_specs, out_specs=out_specs, out_shape=out_shape,
        scratch_shapes=scratch,
        compiler_params=pltpu.CompilerParams(
            dimension_semantics=("arbitrary", "arbitrary"),
            vmem_limit_bytes=VMEM_LIMIT_BYTES),
        name="mixer",
    )(x, st_a, st_ssm, st_b, st_c, *consts)


def _ffn_kernel(x_ref, stf_ref, gffn_ref, wu_ref, wg_ref, cwf_ref, wd_ref, gfin_ref,
                xo_ref, nf_ref, buff, hid_scr, *, n_b, n_t, final_norm):
    t = pl.program_id(1)
    m = n_b * n_t

    @pl.when(t == 0)
    def _load_state():
        _conv_load_tail(buff, stf_ref, CONV_F_W, PAD_F)

    x = x_ref[...].reshape(m, D_MODEL)
    h = (_rms_scale(x) * gffn_ref[...]).astype(BF16)
    up = _dot(h, wu_ref[...])
    _conv_stage(buff, _dot(h, wg_ref[...]).reshape(n_b, n_t, D_FF), CONV_F_W, PAD_F, n_t)

    def emit_f(b, r, y):
        r0 = b * n_t + r
        hid_scr[r0:r0 + y.shape[0], :] = (_silu(y) * up[r0:r0 + y.shape[0]]).astype(BF16)

    _dwconv_blocks(buff, cwf_ref, CONV_F_W, PAD_F, n_b, n_t, SUBLANES, emit_f)
    _carry_tail(buff, nf_ref, CONV_F_W, PAD_F, n_t)

    out = x + _dot(hid_scr[...], wd_ref[...])
    if final_norm:
        out = _rms_scale(out) * gfin_ref[...]
    xo_ref[...] = out.reshape(n_b, n_t, D_MODEL)


def _ffn_call(x, st_f, w, gfin, *, n_b, n_t, final_norm):
    bsz, seq, _ = x.shape
    assert bsz % n_b == 0 and seq % n_t == 0 and n_t % SUBLANES == 0
    m = n_b * n_t
    grid = (bsz // n_b, seq // n_t)
    consts = [w['gffn'], w['wu'], w['wg'], w['cwf'], w['wd'], gfin]
    in_specs = ([pl.BlockSpec((n_b, n_t, D_MODEL), lambda b, t: (b, t, 0)),
                 pl.BlockSpec((n_b, CONV_F_W - 1, D_FF), lambda b, t: (b, 0, 0))]
                + [_const_spec(c.shape) for c in consts])
    out_shape = (jax.ShapeDtypeStruct(x.shape, F32), jax.ShapeDtypeStruct(st_f.shape, F32))
    out_specs = (pl.BlockSpec((n_b, n_t, D_MODEL), lambda b, t: (b, t, 0)),
                 pl.BlockSpec((n_b, CONV_F_W - 1, D_FF), lambda b, t: (b, 0, 0)))
    scratch = [_conv_buf(CONV_F_W, PAD_F, n_b, n_t, D_FF),
               pltpu.VMEM((m, D_FF), BF16)]
    return pl.pallas_call(
        functools.partial(_ffn_kernel, n_b=n_b, n_t=n_t, final_norm=final_norm),
        grid=grid, in_specs=in_specs, out_specs=out_specs, out_shape=out_shape,
        scratch_shapes=scratch,
        compiler_params=pltpu.CompilerParams(
            dimension_semantics=("arbitrary", "arbitrary"),
            vmem_limit_bytes=VMEM_LIMIT_BYTES),
        name="ffn",
    )(x, st_f, *consts)


def _row(v, pad_to=None):
    v = v.astype(F32).reshape(1, -1)
    if pad_to is not None:
        v = jnp.pad(v, ((0, 0), (0, pad_to - v.shape[1])))
    return v


def _layer_weights(i, norm_mix_g, w_in, conv_a_w, conv_b_w, conv_b_bias, dt_bias, a_log, d_skip,
                   ssm_norm_g, conv_c_w, conv_c_bias, ln_c_g, ln_c_b, w_out, norm_ffn_g, w_up,
                   conv_ffn_w, w_down):
    wi = w_in[i]
    c0 = 3 * D_A
    c1 = c0 + D_B
    c2 = c1 + D_XBC
    c3 = c2 + N_HEADS
    return dict(
        gmix=_row(norm_mix_g[i]),
        wa=wi[:, :c0].astype(BF16), wz=wi[:, c0:c1].astype(BF16),
        wxbc=wi[:, c1:c2].astype(BF16),
        wdt=jnp.pad(wi[:, c2:c3], ((0, 0), (0, HEAD_PAD - N_HEADS))).astype(BF16),
        wc=wi[:, c3:].astype(BF16),
        cwa=conv_a_w[i], cwb=conv_b_w[i], cbb=_row(conv_b_bias[i]),
        dtb=_row(dt_bias[i], HEAD_PAD), alog=_row(a_log[i], HEAD_PAD),
        dskip=_row(jnp.repeat(d_skip[i], HEAD_DIM)), gssm=_row(ssm_norm_g[i]),
        cwc=conv_c_w[i], cbc=_row(conv_c_bias[i]), lng=_row(ln_c_g[i]), lnb=_row(ln_c_b[i]),
        wout=w_out[i].astype(BF16),
        gffn=_row(norm_ffn_g[i]),
        wu=w_up[i][:, :D_FF].astype(BF16), wg=w_up[i][:, D_FF:].astype(BF16),
        cwf=conv_ffn_w[i], wd=w_down[i].astype(BF16),
    )


def _layer(x, st_a, st_ssm, st_b, st_c, st_f, w, gfin, *, tile, final_norm):
    bsz = x.shape[0]
    x, na, ns, nb, nc = _mixer_call(x, st_a, st_ssm.reshape(bsz, D_B, D_STATE), st_b, st_c, w,
                                    n_b=tile.mixer_rows, n_t=tile.tokens)
    x, nf = _ffn_call(x, st_f, w, gfin, n_b=tile.ffn_rows, n_t=tile.tokens, final_norm=final_norm)
    return x, (na, ns.reshape(bsz, N_HEADS, HEAD_DIM, D_STATE), nb, nc, nf)


class _Tile(NamedTuple):
    mixer_rows: int
    ffn_rows: int
    tokens: int


PROMPT_TILE = _Tile(1, 1, 256)
SAMPLE_TILE = _Tile(4, 8, 32)


def kernel(x_prompt, x_sample, state_conv_a, state_ssm, state_conv_b, state_conv_c, state_conv_ffn, norm_mix_g, w_in, conv_a_w, conv_b_w, conv_b_bias, dt_bias, a_log, d_skip, ssm_norm_g, conv_c_w, conv_c_bias, ln_c_g, ln_c_b, w_out, norm_ffn_g, w_up, conv_ffn_w, w_down, final_norm_g):
    depth = w_in.shape[0]
    bp = x_prompt.shape[0]
    zeros = (jnp.zeros((bp, CONV_A_W - 1, D_A), F32),
             jnp.zeros((bp, N_HEADS, HEAD_DIM, D_STATE), F32),
             jnp.zeros((bp, CONV_B_W - 1, D_XBC), F32),
             jnp.zeros((bp, CONV_C_W - 1, D_C), F32),
             jnp.zeros((bp, CONV_F_W - 1, D_FF), F32))
    gfin = _row(final_norm_g)
    hp, hs = x_prompt, x_sample
    p_states, s_states = [], []
    for i in range(depth):
        w = _layer_weights(i, norm_mix_g, w_in, conv_a_w, conv_b_w, conv_b_bias, dt_bias, a_log,
                           d_skip, ssm_norm_g, conv_c_w, conv_c_bias, ln_c_g, ln_c_b, w_out,
                           norm_ffn_g, w_up, conv_ffn_w, w_down)
        last = i == depth - 1
        hp, st = _layer(hp, *zeros, w, gfin, tile=PROMPT_TILE, final_norm=last)
        p_states.append(st)
        hs, st = _layer(hs, state_conv_a[i], state_ssm[i], state_conv_b[i], state_conv_c[i],
                        state_conv_ffn[i], w, gfin, tile=SAMPLE_TILE, final_norm=last)
        s_states.append(st)
    p_out = [jnp.stack([s[j] for s in p_states]) for j in range(5)]
    s_out = [jnp.stack([s[j] for s in s_states]) for j in range(5)]
    return (hp, hs, *p_out, *s_out)
```

```python
import functools
from typing import NamedTuple

import jax
import jax.numpy as jnp
from jax import lax
from jax.experimental import pallas as pl
from jax.experimental.pallas import tpu as pltpu

F32 = jnp.float32
BF16 = jnp.bfloat16

D_MODEL = 1024
D_A = 512
D_B = 1024
D_C = 512
N_HEADS = 16
HEAD_DIM = 64
N_GROUPS = 2
HEADS_PER_GROUP = N_HEADS // N_GROUPS
GROUP_W = HEADS_PER_GROUP * HEAD_DIM
D_STATE = 128
D_XBC = D_B + 2 * N_GROUPS * D_STATE
D_FF = 2816
D_MIX = D_A + D_B + D_C
CONV_A_W, CONV_B_W, CONV_C_W, CONV_F_W = 3, 4, 31, 3
EPS = 1e-5
SSD_CHUNK = 64

LANES = 128
SUBLANES = 8
MXU_DIM = 256
HEAD_PAD = LANES
RELAYOUT_MINOR = 1024
VMEM_LIMIT_BYTES = 56 * 1024 * 1024

PAD_A, PAD_B, PAD_C, PAD_F = 8, 8, 32, 8


def _dot(a, b):
    return jnp.dot(a, b, preferred_element_type=F32)


def _dot_nt(a, b):
    return lax.dot_general(a, b, (((1,), (1,)), ((), ())), preferred_element_type=F32)


def _split2(x):
    hi = x.astype(BF16)
    return hi, (x - hi.astype(F32)).astype(BF16)


def _split3(x):
    hi = x.astype(BF16)
    r1 = x - hi.astype(F32)
    mid = r1.astype(BF16)
    return hi, mid, (r1 - mid.astype(F32)).astype(BF16)


def _silu(x):
    return x * jax.nn.sigmoid(x)


def _rms_scale(x):
    return x * lax.rsqrt(jnp.mean(x * x, axis=-1, keepdims=True) + EPS)


def _n_shifts(k):
    return min(k, SUBLANES)


def _conv_buf(k, pad, n_b, n_t, c):
    return pltpu.VMEM((_n_shifts(k), n_b, pad + n_t + SUBLANES, c), F32)


def _conv_load_tail(buf_ref, st, k, pad):
    for v in range(_n_shifts(k)):
        buf_ref[v, :, pad - (k - 1) + v:pad + v, :] = st


def _conv_stage(buf_ref, val, k, pad, n_t):
    for v in range(_n_shifts(k)):
        buf_ref[v, :, pad + v:pad + v + n_t, :] = val


def _dwconv_items(buf_ref, w_ref, k, pad, n_b, n_t, rows, emit):
    c = buf_ref.shape[-1]

    def block(b, r):
        acc = None
        for j in range(k):
            s = pad - (k - 1) + j + r
            v = (-s) % SUBLANES
            win = buf_ref[v, b, s + v:s + v + rows, :].reshape(rows // SUBLANES, SUBLANES, c)
            term = win * w_ref[j]
            acc = term if acc is None else acc + term
        emit(b, r, acc.reshape(rows, c))

    return [functools.partial(block, b, r) for b in range(n_b) for r in range(0, n_t, rows)]


def _interleave(*item_lists):
    keyed = [((i + 0.5) / len(items), n, i, item)
             for n, items in enumerate(item_lists) for i, item in enumerate(items)]
    return [item for _, _, _, item in sorted(keyed, key=lambda t: t[:3])]


def _run(items):
    for item in items:
        item()


def _carry_tail(buf_ref, k, pad, n_t):
    tail = buf_ref[0, :, pad + n_t - (k - 1):pad + n_t, :]
    for v in range(_n_shifts(k)):
        buf_ref[v, :, pad - (k - 1) + v:pad + v, :] = buf_ref[v, :, pad + n_t - (k - 1) + v:pad + n_t + v, :]
    return tail


class _MixerConsts(NamedTuple):
    gmix: object
    wa: object
    wz: object
    wxbc: object
    wc: object
    wdt: object
    cwa: object
    cwb: object
    cbb: object
    dtb: object
    alog: object
    dskip: object
    gssm: object
    cwc: object
    cbc: object
    lng: object
    lnb: object
    wout: object
    e2p: object
    e2q: object


class _MixerScratch(NamedTuple):
    bufa: object
    bufb: object
    bufc: object
    st: object
    xbc: object
    dt: object
    ycat: object
    h: object
    ab: object
    z: object


class _SsdMasks(NamedTuple):
    tril: object
    ones2: object
    causal_rep: object
    eye_rep: object
    blockdiag: object


def _ssd_masks(q):
    hq = N_HEADS * q
    row = lax.broadcasted_iota(jnp.int32, (q, q), 0)
    col = lax.broadcasted_iota(jnp.int32, (q, q), 1)
    rowq = lax.broadcasted_iota(jnp.int32, (q, hq), 0)
    colq = lax.broadcasted_iota(jnp.int32, (q, hq), 1) % q
    heads_per_blk = MXU_DIM // q
    blk_w = heads_per_blk * HEAD_DIM
    bd_r = lax.broadcasted_iota(jnp.int32, (MXU_DIM, blk_w), 0) // q
    bd_c = lax.broadcasted_iota(jnp.int32, (MXU_DIM, blk_w), 1) // HEAD_DIM
    return _SsdMasks(tril=(col <= row).astype(BF16),
                     ones2=jnp.ones((q, 2 * q), BF16),
                     causal_rep=colq <= rowq,
                     eye_rep=(colq == rowq).astype(F32),
                     blockdiag=bd_r == bd_c)


def _ssd_chunk(xbc_c, dt_c, z_c, st, a_row, w, masks, q):
    heads_per_blk = MXU_DIM // q
    blk_w = heads_per_blk * HEAD_DIM
    acs = _dot(masks.tril, jnp.concatenate(_split3(dt_c * a_row), axis=-1))
    a_cum = acs[:, :HEAD_PAD] + acs[:, HEAD_PAD:2 * HEAD_PAD] + acs[:, 2 * HEAD_PAD:]
    a_pieces = jnp.concatenate(_split2(a_cum), axis=-1)
    lhs = jnp.concatenate([a_pieces, jnp.concatenate(_split2(dt_c), axis=-1)], axis=0)
    ex = _dot(lhs, w.e2p[:, :D_B])
    a_x, dt_x = ex[:q], ex[q:]
    a_q = a_x if q == HEAD_DIM else _dot(a_pieces, w.e2q[:, :N_HEADS * q])
    a_row_b = _dot(masks.ones2, jnp.concatenate(_split2(a_q * masks.eye_rep), axis=0))
    decay = jnp.exp(jnp.where(masks.causal_rep, a_q - a_row_b, -jnp.inf))

    xs = xbc_c[:, :D_B]
    bm = xbc_c[:, D_B:D_B + N_GROUPS * D_STATE]
    cm_b = xbc_c[:, D_B + N_GROUPS * D_STATE:].astype(BF16)
    bm_b = bm.astype(BF16)
    cbt = jnp.concatenate(
        [_dot_nt(cm_b[:, g * D_STATE:(g + 1) * D_STATE],
                 jnp.concatenate([bm_b[:, g * D_STATE:(g + 1) * D_STATE]] * HEADS_PER_GROUP, axis=0))
         for g in range(N_GROUPS)], axis=-1)
    m_all = (cbt * decay).astype(BF16)
    xdt = xs * dt_x
    xdt_b = xdt.astype(BF16)
    y_parts = []
    for blk in range(N_HEADS // heads_per_blk):
        xb = xdt_b[:, blk * blk_w:(blk + 1) * blk_w]
        rhs = jnp.where(masks.blockdiag, jnp.concatenate([xb] * heads_per_blk, axis=0), 0)
        y_parts.append(_dot(m_all[:, blk * MXU_DIM:(blk + 1) * MXU_DIM], rhs))
    y_diag = jnp.concatenate(y_parts, axis=-1)

    st_b = st.astype(BF16)
    y_off = jnp.concatenate(
        [_dot(cm_b[:, g * D_STATE:(g + 1) * D_STATE], st_b[:, g * GROUP_W:(g + 1) * GROUP_W])
         for g in range(N_GROUPS)], axis=-1)
    y = y_diag + y_off * jnp.exp(a_x) + w.dskip[...] * xs

    a_last = a_x[q - 1:q, :]
    xw_b = (xdt * jnp.exp(a_last - a_x)).astype(BF16)
    upd = jnp.concatenate(
        [_dot(bm[:, g * D_STATE:(g + 1) * D_STATE].T.astype(BF16), xw_b[:, g * GROUP_W:(g + 1) * GROUP_W])
         for g in range(N_GROUPS)], axis=-1)
    new_st = st * jnp.exp(a_last) + upd
    yg = y * _silu(z_c)
    return (_rms_scale(yg) * w.gssm[...]).astype(BF16), new_st


class _MixerChain:
    def __init__(self, x_ref, xo_ref, na_ref, nb_ref, nc_ref, w, s, masks, b0, rows, n_t, q):
        self.x_ref, self.xo_ref = x_ref, xo_ref
        self.na_ref, self.nb_ref, self.nc_ref = na_ref, nb_ref, nc_ref
        self.w, self.s, self.masks = w, s, masks
        self.b0, self.rows, self.n_t, self.q = b0, rows, n_t, q

    def norm(self):
        x = self.x_ref[self.b0:self.b0 + self.rows].reshape(self.rows * self.n_t, D_MODEL)
        self.s.h[...] = (_rms_scale(x) * self.w.gmix[...]).astype(BF16)

    def project_items(self):
        w, s, rows, n_t = self.w, self.s, self.rows, self.n_t

        def xbc():
            _conv_stage(s.bufb, _dot(s.h[...], w.wxbc[...]).reshape(rows, n_t, D_XBC), CONV_B_W, PAD_B, n_t)
            s.dt[...] = jax.nn.softplus(_dot(s.h[...], w.wdt[...]) + w.dtb[...])

        def a():
            pa = _dot(s.h[...], w.wa[...])
            s.ab[...] = pa[:, D_A:2 * D_A]
            _conv_stage(s.bufa, (pa[:, 2 * D_A:] * pa[:, :D_A]).reshape(rows, n_t, D_A),
                        CONV_A_W, PAD_A, n_t)

        def c():
            pc = _dot(s.h[...], w.wc[:, :2 * D_C])
            _conv_stage(s.bufc, (pc[:, :D_C] * jax.nn.sigmoid(pc[:, D_C:])).reshape(rows, n_t, D_C),
                        CONV_C_W, PAD_C, n_t)

        def z():
            s.z[...] = _dot(s.h[...], w.wz[:, :D_B])

        return [xbc, a, c, z]

    def conv_items(self):
        w, s, b0, rows, n_t = self.w, self.s, self.b0, self.rows, self.n_t

        def emit_b(b, r, y):
            r0 = b * n_t + r
            s.xbc[r0:r0 + y.shape[0], :] = _silu(y + w.cbb[...])

        def emit_a(b, r, y):
            r0 = b * n_t + r
            s.ycat[r0:r0 + y.shape[0], 0:D_A] = (s.ab[r0:r0 + y.shape[0], :] * y).astype(BF16)

        def emit_c(b, r, y):
            r0 = b * n_t + r
            y = y + w.cbc[...]
            mu = jnp.mean(y, axis=-1, keepdims=True)
            yc = y - mu
            var = jnp.mean(yc * yc, axis=-1, keepdims=True)
            yn = yc * lax.rsqrt(var + EPS) * w.lng[...] + w.lnb[...]
            s.ycat[r0:r0 + y.shape[0], D_A + D_B:D_MIX] = _silu(yn).astype(BF16)

        def tail_b():
            self.nb_ref[b0:b0 + rows] = _carry_tail(s.bufb, CONV_B_W, PAD_B, n_t)

        def tail_a():
            self.na_ref[b0:b0 + rows] = _carry_tail(s.bufa, CONV_A_W, PAD_A, n_t)

        def tail_c():
            self.nc_ref[b0:b0 + rows] = _carry_tail(s.bufc, CONV_C_W, PAD_C, n_t)

        return (_dwconv_items(s.bufb, w.cwb, CONV_B_W, PAD_B, rows, n_t, min(n_t, 16), emit_b) + [tail_b]
                + _dwconv_items(s.bufa, w.cwa, CONV_A_W, PAD_A, rows, n_t, min(n_t, 32), emit_a) + [tail_a]
                + _dwconv_items(s.bufc, w.cwc, CONV_C_W, PAD_C, rows, n_t, min(n_t, 32), emit_c) + [tail_c])

    def ssd_items(self):
        w, s, n_t, q = self.w, self.s, self.n_t, self.q

        def chunk(b, c):
            r0 = b * n_t + c * q
            a_row = -jnp.exp(w.alog[...])
            y_b, new_st = _ssd_chunk(s.xbc[r0:r0 + q, :], s.dt[r0:r0 + q, :], s.z[r0:r0 + q, :],
                                     s.st[b], a_row, w, self.masks, q)
            s.st[b] = new_st
            s.ycat[r0:r0 + q, D_A:D_A + D_B] = y_b

        def out():
            y = _dot(s.ycat[...], w.wout[:, :D_MODEL]).reshape(self.rows, n_t, D_MODEL)
            self.xo_ref[self.b0:self.b0 + self.rows] = self.x_ref[self.b0:self.b0 + self.rows] + y

        return [functools.partial(chunk, b, c) for b in range(self.rows) for c in range(n_t // q)] + [out]


N_MIXER_STATE_IN = 4


def _mixer_kernel(*refs, n_b, n_t, q, n_chain):
    x_ref, sta_ref, sts_ref, stb_ref, stc_ref = refs[:1 + N_MIXER_STATE_IN]
    n_in = 1 + N_MIXER_STATE_IN + len(_MixerConsts._fields)
    w = _MixerConsts(*refs[1 + N_MIXER_STATE_IN:n_in])
    xo_ref, na_ref, ns_ref, nb_ref, nc_ref = refs[n_in:n_in + 5]
    flat = refs[n_in + 5:]
    per = len(_MixerScratch._fields)
    chains = [_MixerScratch(*flat[c * per:(c + 1) * per]) for c in range(n_chain)]
    rows = n_b // n_chain
    t = pl.program_id(1)

    @pl.when(t == 0)
    def _load_state():
        for c, s in enumerate(chains):
            b0 = c * rows
            _conv_load_tail(s.bufa, sta_ref[b0:b0 + rows], CONV_A_W, PAD_A)
            _conv_load_tail(s.bufb, stb_ref[b0:b0 + rows], CONV_B_W, PAD_B)
            _conv_load_tail(s.bufc, stc_ref[b0:b0 + rows], CONV_C_W, PAD_C)
            for b in range(rows):
                s.st[b] = sts_ref[b0 + b].T

    masks = _ssd_masks(q)
    cs = [_MixerChain(x_ref, xo_ref, na_ref, nb_ref, nc_ref, w, s, masks, c * rows, rows, n_t, q)
          for c, s in enumerate(chains)]
    for chain in cs:
        chain.norm()
    _run(cs[0].project_items())
    for k in range(n_chain + 1):
        _run(_interleave(*([cs[k].conv_items()] if k < n_chain else []),
                         *([cs[k + 1].project_items()] if k + 1 < n_chain else []),
                         *([cs[k - 1].ssd_items()] if k > 0 else [])))

    @pl.when(t == pl.num_programs(1) - 1)
    def _store_state():
        for c, s in enumerate(chains):
            for b in range(rows):
                ns_ref[c * rows + b] = s.st[b].T


def _expand_matrix(width):
    head = jnp.arange(HEAD_PAD)[:, None]
    lane_head = (jnp.arange(N_HEADS * width) // width)[None, :]
    e = (head == lane_head).astype(F32)
    return _mxu_weight(jnp.concatenate([e, e], axis=0))


def _const_spec(shape):
    nd = len(shape)
    return pl.BlockSpec(shape, lambda b, t: (0,) * nd, pipeline_mode=pl.Buffered(1))


def _mixer_call(x, st_a, st_ssm, st_b, st_c, w, *, n_b, n_t, n_chain):
    bsz, seq, _ = x.shape
    q = min(SSD_CHUNK, n_t)
    assert bsz % n_b == 0 and seq % n_t == 0 and n_t % q == 0 and MXU_DIM % q == 0
    assert n_b % n_chain == 0 and n_t >= CONV_C_W - 1 and n_t % SUBLANES == 0
    rows = n_b // n_chain
    m = rows * n_t
    grid = (bsz // n_b, seq // n_t)

    def bspec(shape):
        return pl.BlockSpec((n_b,) + shape, lambda b, t: (b, 0, 0))

    consts = _MixerConsts(e2p=_expand_matrix(HEAD_DIM), e2q=_expand_matrix(q),
                          **{k: w[k] for k in _MixerConsts._fields if k not in ('e2p', 'e2q')})
    in_specs = ([pl.BlockSpec((n_b, n_t, D_MODEL), lambda b, t: (b, t, 0)),
                 bspec((CONV_A_W - 1, D_A)), bspec((D_B, D_STATE)),
                 bspec((CONV_B_W - 1, D_XBC)), bspec((CONV_C_W - 1, D_C))]
                + [_const_spec(c.shape) for c in consts])
    out_shape = (jax.ShapeDtypeStruct(x.shape, F32),
                 jax.ShapeDtypeStruct(st_a.shape, F32),
                 jax.ShapeDtypeStruct(st_ssm.shape, F32),
                 jax.ShapeDtypeStruct(st_b.shape, F32),
                 jax.ShapeDtypeStruct(st_c.shape, F32))
    out_specs = (pl.BlockSpec((n_b, n_t, D_MODEL), lambda b, t: (b, t, 0)),
                 bspec((CONV_A_W - 1, D_A)), bspec((D_B, D_STATE)),
                 bspec((CONV_B_W - 1, D_XBC)), bspec((CONV_C_W - 1, D_C)))
    scratch = list(_MixerScratch(
        bufa=_conv_buf(CONV_A_W, PAD_A, rows, n_t, D_A),
        bufb=_conv_buf(CONV_B_W, PAD_B, rows, n_t, D_XBC),
        bufc=_conv_buf(CONV_C_W, PAD_C, rows, n_t, D_C),
        st=pltpu.VMEM((rows, D_STATE, D_B), F32),
        xbc=pltpu.VMEM((m, D_XBC), F32),
        dt=pltpu.VMEM((m, HEAD_PAD), F32),
        ycat=pltpu.VMEM((m, D_MIX), BF16),
        h=pltpu.VMEM((m, D_MODEL), BF16),
        ab=pltpu.VMEM((m, D_A), F32),
        z=pltpu.VMEM((m, D_B), F32))) * n_chain
    return pl.pallas_call(
        functools.partial(_mixer_kernel, n_b=n_b, n_t=n_t, q=q, n_chain=n_chain),
        grid=grid, in_specs=in_specs, out_specs=out_specs, out_shape=out_shape,
        scratch_shapes=scratch,
        compiler_params=pltpu.CompilerParams(
            dimension_semantics=("arbitrary", "arbitrary"),
            vmem_limit_bytes=VMEM_LIMIT_BYTES),
        name="mixer",
    )(x, st_a, st_ssm, st_b, st_c, *consts)


class _FfnConsts(NamedTuple):
    gffn: object
    wu: object
    wg: object
    cwf: object
    wd: object
    gfin: object


class _FfnScratch(NamedTuple):
    buff: object
    hid: object


def _ffn_project(x_ref, w, s, b0, rows, n_t):
    m = rows * n_t
    x = x_ref[b0:b0 + rows].reshape(m, D_MODEL)
    h = (_rms_scale(x) * w.gffn[...]).astype(BF16)
    up = _dot(h, w.wu[...])
    _conv_stage(s.buff, _dot(h, w.wg[...]).reshape(rows, n_t, D_FF), CONV_F_W, PAD_F, n_t)
    return x, up


def _ffn_finish(x, up, xo_ref, nf_ref, w, s, b0, rows, n_t, final_norm):
    def emit_f(b, r, y):
        r0 = b * n_t + r
        s.hid[r0:r0 + y.shape[0], :] = (_silu(y) * up[r0:r0 + y.shape[0]]).astype(BF16)

    _run(_dwconv_items(s.buff, w.cwf, CONV_F_W, PAD_F, rows, n_t, SUBLANES, emit_f))
    nf_ref[b0:b0 + rows] = _carry_tail(s.buff, CONV_F_W, PAD_F, n_t)

    out = x + _dot(s.hid[...], w.wd[:, :D_MODEL])
    if final_norm:
        out = _rms_scale(out) * w.gfin[...]
    xo_ref[b0:b0 + rows] = out.reshape(rows, n_t, D_MODEL)


def _ffn_kernel(*refs, n_b, n_t, n_chain, final_norm):
    x_ref, stf_ref = refs[:2]
    n_in = 2 + len(_FfnConsts._fields)
    w = _FfnConsts(*refs[2:n_in])
    xo_ref, nf_ref = refs[n_in:n_in + 2]
    flat = refs[n_in + 2:]
    per = len(_FfnScratch._fields)
    chains = [_FfnScratch(*flat[c * per:(c + 1) * per]) for c in range(n_chain)]
    rows = n_b // n_chain

    @pl.when(pl.program_id(1) == 0)
    def _load_state():
        for c, s in enumerate(chains):
            _conv_load_tail(s.buff, stf_ref[c * rows:(c + 1) * rows], CONV_F_W, PAD_F)

    staged = [_ffn_project(x_ref, w, s, c * rows, rows, n_t) for c, s in enumerate(chains)]
    for c, s in enumerate(chains):
        x, up = staged[c]
        _ffn_finish(x, up, xo_ref, nf_ref, w, s, c * rows, rows, n_t, final_norm)


def _ffn_call(x, st_f, w, gfin, *, n_b, n_t, n_chain, final_norm):
    bsz, seq, _ = x.shape
    assert bsz % n_b == 0 and seq % n_t == 0 and n_t % SUBLANES == 0 and n_b % n_chain == 0
    rows = n_b // n_chain
    m = rows * n_t
    grid = (bsz // n_b, seq // n_t)
    consts = _FfnConsts(gfin=gfin, **{k: w[k] for k in _FfnConsts._fields if k != 'gfin'})
    in_specs = ([pl.BlockSpec((n_b, n_t, D_MODEL), lambda b, t: (b, t, 0)),
                 pl.BlockSpec((n_b, CONV_F_W - 1, D_FF), lambda b, t: (b, 0, 0))]
                + [_const_spec(c.shape) for c in consts])
    out_shape = (jax.ShapeDtypeStruct(x.shape, F32), jax.ShapeDtypeStruct(st_f.shape, F32))
    out_specs = (pl.BlockSpec((n_b, n_t, D_MODEL), lambda b, t: (b, t, 0)),
                 pl.BlockSpec((n_b, CONV_F_W - 1, D_FF), lambda b, t: (b, 0, 0)))
    scratch = list(_FfnScratch(buff=_conv_buf(CONV_F_W, PAD_F, rows, n_t, D_FF),
                               hid=pltpu.VMEM((m, D_FF), BF16))) * n_chain
    return pl.pallas_call(
        functools.partial(_ffn_kernel, n_b=n_b, n_t=n_t, n_chain=n_chain, final_norm=final_norm),
        grid=grid, in_specs=in_specs, out_specs=out_specs, out_shape=out_shape,
        scratch_shapes=scratch,
        compiler_params=pltpu.CompilerParams(
            dimension_semantics=("arbitrary", "arbitrary"),
            vmem_limit_bytes=VMEM_LIMIT_BYTES),
        name="ffn",
    )(x, st_f, *consts)


def _row(v, pad_to=None):
    v = v.astype(F32).reshape(1, -1)
    if pad_to is not None:
        v = jnp.pad(v, ((0, 0), (0, pad_to - v.shape[1])))
    return v


def _mxu_weight(w):
    w = w.astype(BF16)
    if w.shape[-1] % RELAYOUT_MINOR == 0:
        w = jnp.pad(w, ((0, 0), (0, LANES)))
    return w


def _taps(w):
    return jnp.broadcast_to(w.astype(F32)[:, None, :], (w.shape[0], SUBLANES, w.shape[1]))


def _layer_weights(i, norm_mix_g, w_in, conv_a_w, conv_b_w, conv_b_bias, dt_bias, a_log, d_skip,
                   ssm_norm_g, conv_c_w, conv_c_bias, ln_c_g, ln_c_b, w_out, norm_ffn_g, w_up,
                   conv_ffn_w, w_down):
    wi = w_in[i]
    c0 = 3 * D_A
    c1 = c0 + D_B
    c2 = c1 + D_XBC
    c3 = c2 + N_HEADS
    return dict(
        gmix=_row(norm_mix_g[i]),
        wa=_mxu_weight(wi[:, :c0]), wz=_mxu_weight(wi[:, c0:c1]),
        wxbc=_mxu_weight(wi[:, c1:c2]),
        wdt=jnp.pad(wi[:, c2:c3], ((0, 0), (0, HEAD_PAD - N_HEADS))).astype(BF16),
        wc=_mxu_weight(wi[:, c3:]),
        cwa=_taps(conv_a_w[i]), cwb=_taps(conv_b_w[i]), cbb=_row(conv_b_bias[i]),
        dtb=_row(dt_bias[i], HEAD_PAD), alog=_row(a_log[i], HEAD_PAD),
        dskip=_row(jnp.repeat(d_skip[i], HEAD_DIM)), gssm=_row(ssm_norm_g[i]),
        cwc=_taps(conv_c_w[i]), cbc=_row(conv_c_bias[i]), lng=_row(ln_c_g[i]), lnb=_row(ln_c_b[i]),
        wout=_mxu_weight(w_out[i]),
        gffn=_row(norm_ffn_g[i]),
        wu=_mxu_weight(w_up[i][:, :D_FF]), wg=_mxu_weight(w_up[i][:, D_FF:]),
        cwf=_taps(conv_ffn_w[i]), wd=_mxu_weight(w_down[i]),
    )


class _Tile(NamedTuple):
    rows: int
    tokens: int
    chains: int


class _StreamTiles(NamedTuple):
    mixer: _Tile
    ffn: _Tile


PROMPT_TILES = _StreamTiles(mixer=_Tile(2, 128, 2), ffn=_Tile(2, 128, 2))
SAMPLE_TILES = _StreamTiles(mixer=_Tile(4, 32, 1), ffn=_Tile(8, 32, 1))


def _layer(x, st_a, st_ssm, st_b, st_c, st_f, w, gfin, *, tiles, final_norm):
    bsz = x.shape[0]
    tm, tf = tiles.mixer, tiles.ffn
    x, na, ns, nb, nc = _mixer_call(x, st_a, st_ssm.reshape(bsz, D_B, D_STATE), st_b, st_c, w,
                                    n_b=tm.rows, n_t=tm.tokens, n_chain=tm.chains)
    x, nf = _ffn_call(x, st_f, w, gfin, n_b=tf.rows, n_t=tf.tokens, n_chain=tf.chains,
                      final_norm=final_norm)
    return x, (na, ns.reshape(bsz, N_HEADS, HEAD_DIM, D_STATE), nb, nc, nf)


def kernel(x_prompt, x_sample, state_conv_a, state_ssm, state_conv_b, state_conv_c, state_conv_ffn, norm_mix_g, w_in, conv_a_w, conv_b_w, conv_b_bias, dt_bias, a_log, d_skip, ssm_norm_g, conv_c_w, conv_c_bias, ln_c_g, ln_c_b, w_out, norm_ffn_g, w_up, conv_ffn_w, w_down, final_norm_g):
    depth = w_in.shape[0]
    bp = x_prompt.shape[0]
    zeros = (jnp.zeros((bp, CONV_A_W - 1, D_A), F32),
             jnp.zeros((bp, N_HEADS, HEAD_DIM, D_STATE), F32),
             jnp.zeros((bp, CONV_B_W - 1, D_XBC), F32),
             jnp.zeros((bp, CONV_C_W - 1, D_C), F32),
             jnp.zeros((bp, CONV_F_W - 1, D_FF), F32))
    gfin = _row(final_norm_g)
    hp, hs = x_prompt, x_sample
    p_states, s_states = [], []
    for i in range(depth):
        w = _layer_weights(i, norm_mix_g, w_in, conv_a_w, conv_b_w, conv_b_bias, dt_bias, a_log,
                           d_skip, ssm_norm_g, conv_c_w, conv_c_bias, ln_c_g, ln_c_b, w_out,
                           norm_ffn_g, w_up, conv_ffn_w, w_down)
        last = i == depth - 1
        hp, st = _layer(hp, *zeros, w, gfin, tiles=PROMPT_TILES, final_norm=last)
        p_states.append(st)
        hs, st = _layer(hs, state_conv_a[i], state_ssm[i], state_conv_b[i], state_conv_c[i],
                        state_conv_ffn[i], w, gfin, tiles=SAMPLE_TILES, final_norm=last)
        s_states.append(st)
    p_out = [jnp.stack([s[j] for s in p_states]) for j in range(5)]
    s_out = [jnp.stack([s[j] for s in s_states]) for j in range(5)]
    return (hp, hs, *p_out, *s_out)
```

```python
import functools
from typing import NamedTuple

import jax
import jax.numpy as jnp
from jax import lax
from jax.experimental import pallas as pl
from jax.experimental.pallas import tpu as pltpu

F32 = jnp.float32
BF16 = jnp.bfloat16

D_MODEL = 1024
D_A = 512
D_B = 1024
D_C = 512
N_HEADS = 16
HEAD_DIM = 64
N_GROUPS = 2
HEADS_PER_GROUP = N_HEADS // N_GROUPS
GROUP_W = HEADS_PER_GROUP * HEAD_DIM
D_STATE = 128
D_XBC = D_B + 2 * N_GROUPS * D_STATE
D_FF = 2816
D_MIX = D_A + D_B + D_C
CONV_A_W, CONV_B_W, CONV_C_W, CONV_F_W = 3, 4, 31, 3
EPS = 1e-5
SSD_CHUNK = 64

LANES = 128
SUBLANES = 8
MXU_DIM = 256
HEAD_PAD = LANES
RELAYOUT_MINOR = 1024
VMEM_LIMIT_BYTES = 56 * 1024 * 1024

PAD_A, PAD_B, PAD_C, PAD_F = 8, 8, 32, 8


def _dot(a, b):
    return jnp.dot(a, b, preferred_element_type=F32)


def _dot_nt(a, b):
    return lax.dot_general(a, b, (((1,), (1,)), ((), ())), preferred_element_type=F32)


def _split2(x):
    hi = x.astype(BF16)
    return hi, (x - hi.astype(F32)).astype(BF16)


def _split3(x):
    hi = x.astype(BF16)
    r1 = x - hi.astype(F32)
    mid = r1.astype(BF16)
    return hi, mid, (r1 - mid.astype(F32)).astype(BF16)


def _silu(x):
    return x * jax.nn.sigmoid(x)


def _rms_scale(x):
    return x * lax.rsqrt(jnp.mean(x * x, axis=-1, keepdims=True) + EPS)


def _n_shifts(k):
    return min(k, SUBLANES)


def _conv_buf(k, pad, n_b, n_t, c):
    return pltpu.VMEM((_n_shifts(k), n_b, pad + n_t + SUBLANES, c), F32)


def _conv_load_tail(buf_ref, st, k, pad):
    for v in range(_n_shifts(k)):
        buf_ref[v, :, pad - (k - 1) + v:pad + v, :] = st


def _conv_stage(buf_ref, val, k, pad, n_t):
    for v in range(_n_shifts(k)):
        buf_ref[v, :, pad + v:pad + v + n_t, :] = val


def _dwconv_items(buf_ref, w_ref, k, pad, n_b, n_t, rows, emit):
    c = buf_ref.shape[-1]

    def block(b, r):
        acc = None
        for j in range(k):
            s = pad - (k - 1) + j + r
            v = (-s) % SUBLANES
            win = buf_ref[v, b, s + v:s + v + rows, :].reshape(rows // SUBLANES, SUBLANES, c)
            term = win * w_ref[j]
            acc = term if acc is None else acc + term
        emit(b, r, acc.reshape(rows, c))

    return [functools.partial(block, b, r) for b in range(n_b) for r in range(0, n_t, rows)]


def _interleave(*item_lists):
    keyed = [((i + 0.5) / len(items), n, i, item)
             for n, items in enumerate(item_lists) for i, item in enumerate(items)]
    return [item for _, _, _, item in sorted(keyed, key=lambda t: t[:3])]


def _run(items):
    for item in items:
        item()


def _carry_tail(buf_ref, k, pad, n_t):
    tail = buf_ref[0, :, pad + n_t - (k - 1):pad + n_t, :]
    for v in range(_n_shifts(k)):
        buf_ref[v, :, pad - (k - 1) + v:pad + v, :] = buf_ref[v, :, pad + n_t - (k - 1) + v:pad + n_t + v, :]
    return tail


class _MixerConsts(NamedTuple):
    gmix: object
    wa: object
    wz: object
    wxbc: object
    wc: object
    wdt: object
    cwa: object
    cwb: object
    cbb: object
    dtb: object
    alog: object
    dskip: object
    gssm: object
    cwc: object
    cbc: object
    lng: object
    lnb: object
    wout: object
    e2p: object
    e2q: object


class _MixerScratch(NamedTuple):
    bufa: object
    bufb: object
    bufc: object
    st: object
    xbc: object
    dt: object
    ycat: object
    h: object
    ab: object
    z: object


class _SsdMasks(NamedTuple):
    tril: object
    ones2: object
    causal_rep: object
    eye_rep: object
    blockdiag: object


def _ssd_masks(q):
    hq = N_HEADS * q
    row = lax.broadcasted_iota(jnp.int32, (q, q), 0)
    col = lax.broadcasted_iota(jnp.int32, (q, q), 1)
    rowq = lax.broadcasted_iota(jnp.int32, (q, hq), 0)
    colq = lax.broadcasted_iota(jnp.int32, (q, hq), 1) % q
    heads_per_blk = MXU_DIM // q
    blk_w = heads_per_blk * HEAD_DIM
    bd_r = lax.broadcasted_iota(jnp.int32, (MXU_DIM, blk_w), 0) // q
    bd_c = lax.broadcasted_iota(jnp.int32, (MXU_DIM, blk_w), 1) // HEAD_DIM
    return _SsdMasks(tril=(col <= row).astype(BF16),
                     ones2=jnp.ones((q, 2 * q), BF16),
                     causal_rep=colq <= rowq,
                     eye_rep=(colq == rowq).astype(F32),
                     blockdiag=bd_r == bd_c)


def _group(x, g, width):
    return x[:, g * width:(g + 1) * width]


def _ssd_scan(w, s, masks, rows, n_t, q):
    m = rows * n_t
    chunks_per_row = n_t // q
    chunk_rows = [slice(c * q, (c + 1) * q) for c in range(m // q)]
    heads_per_blk = MXU_DIM // q
    blk_w = heads_per_blk * HEAD_DIM
    a_row = -jnp.exp(w.alog[...])
    dt = s.dt[...]
    dta = dt * a_row

    def cumsum(rc):
        acs = _dot(masks.tril, jnp.concatenate(_split3(dta[rc]), axis=-1))
        return acs[:, :HEAD_PAD] + acs[:, HEAD_PAD:2 * HEAD_PAD] + acs[:, 2 * HEAD_PAD:]

    a_cum = jnp.concatenate([cumsum(rc) for rc in chunk_rows], axis=0)
    a_pieces = jnp.concatenate(_split2(a_cum), axis=-1)
    ex = _dot(jnp.concatenate([a_pieces, jnp.concatenate(_split2(dt), axis=-1)], axis=0), w.e2p[:, :D_B])
    a_x, dt_x = ex[:m], ex[m:]
    a_q = a_x if q == HEAD_DIM else _dot(a_pieces, w.e2q[:, :N_HEADS * q])

    xbc = s.xbc[...]
    xs = xbc[:, :D_B]
    bm = xbc[:, D_B:D_B + N_GROUPS * D_STATE]
    cm_b = xbc[:, D_B + N_GROUPS * D_STATE:].astype(BF16)
    bm_b = bm.astype(BF16)
    xdt = xs * dt_x
    xdt_b = xdt.astype(BF16)

    def diag_block(rc):
        a_row_b = _dot(masks.ones2, jnp.concatenate(_split2(a_q[rc] * masks.eye_rep), axis=0))
        decay = jnp.exp(jnp.where(masks.causal_rep, a_q[rc] - a_row_b, -jnp.inf))
        cbt = jnp.concatenate(
            [_dot_nt(_group(cm_b[rc], g, D_STATE),
                     jnp.concatenate([_group(bm_b[rc], g, D_STATE)] * HEADS_PER_GROUP, axis=0))
             for g in range(N_GROUPS)], axis=-1)
        m_all = (cbt * decay).astype(BF16)
        parts = []
        for blk in range(N_HEADS // heads_per_blk):
            xb = _group(xdt_b[rc], blk, blk_w)
            rhs = jnp.where(masks.blockdiag, jnp.concatenate([xb] * heads_per_blk, axis=0), 0)
            parts.append(_dot(_group(m_all, blk, MXU_DIM), rhs))
        return jnp.concatenate(parts, axis=-1)

    y_diag = jnp.concatenate([diag_block(rc) for rc in chunk_rows], axis=0)

    a_last = [a_x[rc.stop - 1:rc.stop, :] for rc in chunk_rows]

    def state_update(c, rc):
        xw_b = (xdt[rc] * jnp.exp(a_last[c] - a_x[rc])).astype(BF16)
        return jnp.concatenate(
            [_dot(_group(bm[rc], g, D_STATE).T.astype(BF16), _group(xw_b, g, GROUP_W))
             for g in range(N_GROUPS)], axis=-1)

    upd = [state_update(c, rc) for c, rc in enumerate(chunk_rows)]

    y_off = []
    for b in range(rows):
        st = s.st[b]
        for j in range(chunks_per_row):
            c = b * chunks_per_row + j
            st_b = st.astype(BF16)
            y_off.append(jnp.concatenate(
                [_dot(_group(cm_b[chunk_rows[c]], g, D_STATE), _group(st_b, g, GROUP_W))
                 for g in range(N_GROUPS)], axis=-1))
            st = st * jnp.exp(a_last[c]) + upd[c]
        s.st[b] = st
    y = y_diag + jnp.concatenate(y_off, axis=0) * jnp.exp(a_x) + w.dskip[...] * xs
    yg = y * _silu(s.z[...])
    s.ycat[:, D_A:D_A + D_B] = (_rms_scale(yg) * w.gssm[...]).astype(BF16)


class _MixerChain:
    def __init__(self, x_ref, xo_ref, na_ref, nb_ref, nc_ref, w, s, masks, b0, rows, n_t, q):
        self.x_ref, self.xo_ref = x_ref, xo_ref
        self.na_ref, self.nb_ref, self.nc_ref = na_ref, nb_ref, nc_ref
        self.w, self.s, self.masks = w, s, masks
        self.b0, self.rows, self.n_t, self.q = b0, rows, n_t, q

    def norm(self):
        x = self.x_ref[self.b0:self.b0 + self.rows].reshape(self.rows * self.n_t, D_MODEL)
        self.s.h[...] = (_rms_scale(x) * self.w.gmix[...]).astype(BF16)

    def project_items(self):
        w, s, rows, n_t = self.w, self.s, self.rows, self.n_t

        def xbc():
            _conv_stage(s.bufb, _dot(s.h[...], w.wxbc[...]).reshape(rows, n_t, D_XBC), CONV_B_W, PAD_B, n_t)
            s.dt[...] = jax.nn.softplus(_dot(s.h[...], w.wdt[...]) + w.dtb[...])

        def a():
            pa = _dot(s.h[...], w.wa[...])
            s.ab[...] = pa[:, D_A:2 * D_A]
            _conv_stage(s.bufa, (pa[:, 2 * D_A:] * pa[:, :D_A]).reshape(rows, n_t, D_A),
                        CONV_A_W, PAD_A, n_t)

        def c():
            pc = _dot(s.h[...], w.wc[:, :2 * D_C])
            _conv_stage(s.bufc, (pc[:, :D_C] * jax.nn.sigmoid(pc[:, D_C:])).reshape(rows, n_t, D_C),
                        CONV_C_W, PAD_C, n_t)

        def z():
            s.z[...] = _dot(s.h[...], w.wz[:, :D_B])

        return [xbc, a, c, z]

    def conv_items(self):
        w, s, b0, rows, n_t = self.w, self.s, self.b0, self.rows, self.n_t

        def emit_b(b, r, y):
            r0 = b * n_t + r
            s.xbc[r0:r0 + y.shape[0], :] = _silu(y + w.cbb[...])

        def emit_a(b, r, y):
            r0 = b * n_t + r
            s.ycat[r0:r0 + y.shape[0], 0:D_A] = (s.ab[r0:r0 + y.shape[0], :] * y).astype(BF16)

        def emit_c(b, r, y):
            r0 = b * n_t + r
            y = y + w.cbc[...]
            mu = jnp.mean(y, axis=-1, keepdims=True)
            yc = y - mu
            var = jnp.mean(yc * yc, axis=-1, keepdims=True)
            yn = yc * lax.rsqrt(var + EPS) * w.lng[...] + w.lnb[...]
            s.ycat[r0:r0 + y.shape[0], D_A + D_B:D_MIX] = _silu(yn).astype(BF16)

        def tail_b():
            self.nb_ref[b0:b0 + rows] = _carry_tail(s.bufb, CONV_B_W, PAD_B, n_t)

        def tail_a():
            self.na_ref[b0:b0 + rows] = _carry_tail(s.bufa, CONV_A_W, PAD_A, n_t)

        def tail_c():
            self.nc_ref[b0:b0 + rows] = _carry_tail(s.bufc, CONV_C_W, PAD_C, n_t)

        return (_dwconv_items(s.bufb, w.cwb, CONV_B_W, PAD_B, rows, n_t, min(n_t, 16), emit_b) + [tail_b]
                + _dwconv_items(s.bufa, w.cwa, CONV_A_W, PAD_A, rows, n_t, min(n_t, 32), emit_a) + [tail_a]
                + _dwconv_items(s.bufc, w.cwc, CONV_C_W, PAD_C, rows, n_t, min(n_t, 32), emit_c) + [tail_c])

    def ssd_items(self):
        w, s, n_t = self.w, self.s, self.n_t

        def scan():
            _ssd_scan(w, s, self.masks, self.rows, n_t, self.q)

        def out():
            y = _dot(s.ycat[...], w.wout[:, :D_MODEL]).reshape(self.rows, n_t, D_MODEL)
            self.xo_ref[self.b0:self.b0 + self.rows] = self.x_ref[self.b0:self.b0 + self.rows] + y

        return [scan, out]


N_MIXER_STATE_IN = 4


def _mixer_kernel(*refs, n_b, n_t, q, n_chain):
    x_ref, sta_ref, sts_ref, stb_ref, stc_ref = refs[:1 + N_MIXER_STATE_IN]
    n_in = 1 + N_MIXER_STATE_IN + len(_MixerConsts._fields)
    w = _MixerConsts(*refs[1 + N_MIXER_STATE_IN:n_in])
    xo_ref, na_ref, ns_ref, nb_ref, nc_ref = refs[n_in:n_in + 5]
    flat = refs[n_in + 5:]
    per = len(_MixerScratch._fields)
    chains = [_MixerScratch(*flat[c * per:(c + 1) * per]) for c in range(n_chain)]
    rows = n_b // n_chain
    t = pl.program_id(1)

    @pl.when(t == 0)
    def _load_state():
        for c, s in enumerate(chains):
            b0 = c * rows
            _conv_load_tail(s.bufa, sta_ref[b0:b0 + rows], CONV_A_W, PAD_A)
            _conv_load_tail(s.bufb, stb_ref[b0:b0 + rows], CONV_B_W, PAD_B)
            _conv_load_tail(s.bufc, stc_ref[b0:b0 + rows], CONV_C_W, PAD_C)
            for b in range(rows):
                s.st[b] = sts_ref[b0 + b].T

    masks = _ssd_masks(q)
    cs = [_MixerChain(x_ref, xo_ref, na_ref, nb_ref, nc_ref, w, s, masks, c * rows, rows, n_t, q)
          for c, s in enumerate(chains)]
    for chain in cs:
        chain.norm()
    _run(cs[0].project_items())
    for k in range(n_chain + 1):
        _run(_interleave(*([cs[k].conv_items()] if k < n_chain else []),
                         *([cs[k + 1].project_items()] if k + 1 < n_chain else []),
                         *([cs[k - 1].ssd_items()] if k > 0 else [])))

    @pl.when(t == pl.num_programs(1) - 1)
    def _store_state():
        for c, s in enumerate(chains):
            for b in range(rows):
                ns_ref[c * rows + b] = s.st[b].T


def _expand_matrix(width):
    head = jnp.arange(HEAD_PAD)[:, None]
    lane_head = (jnp.arange(N_HEADS * width) // width)[None, :]
    e = (head == lane_head).astype(F32)
    return _mxu_weight(jnp.concatenate([e, e], axis=0))


def _const_spec(shape):
    nd = len(shape)
    return pl.BlockSpec(shape, lambda b, t: (0,) * nd, pipeline_mode=pl.Buffered(1))


def _mixer_call(x, st_a, st_ssm, st_b, st_c, w, *, n_b, n_t, n_chain):
    bsz, seq, _ = x.shape
    q = min(SSD_CHUNK, n_t)
    assert bsz % n_b == 0 and seq % n_t == 0 and n_t % q == 0 and MXU_DIM % q == 0
    assert n_b % n_chain == 0 and n_t >= CONV_C_W - 1 and n_t % SUBLANES == 0
    rows = n_b // n_chain
    m = rows * n_t
    grid = (bsz // n_b, seq // n_t)

    def bspec(shape):
        return pl.BlockSpec((n_b,) + shape, lambda b, t: (b, 0, 0))

    consts = _MixerConsts(e2p=_expand_matrix(HEAD_DIM), e2q=_expand_matrix(q),
                          **{k: w[k] for k in _MixerConsts._fields if k not in ('e2p', 'e2q')})
    in_specs = ([pl.BlockSpec((n_b, n_t, D_MODEL), lambda b, t: (b, t, 0)),
                 bspec((CONV_A_W - 1, D_A)), bspec((D_B, D_STATE)),
                 bspec((CONV_B_W - 1, D_XBC)), bspec((CONV_C_W - 1, D_C))]
                + [_const_spec(c.shape) for c in consts])
    out_shape = (jax.ShapeDtypeStruct(x.shape, F32),
                 jax.ShapeDtypeStruct(st_a.shape, F32),
                 jax.ShapeDtypeStruct(st_ssm.shape, F32),
                 jax.ShapeDtypeStruct(st_b.shape, F32),
                 jax.ShapeDtypeStruct(st_c.shape, F32))
    out_specs = (pl.BlockSpec((n_b, n_t, D_MODEL), lambda b, t: (b, t, 0)),
                 bspec((CONV_A_W - 1, D_A)), bspec((D_B, D_STATE)),
                 bspec((CONV_B_W - 1, D_XBC)), bspec((CONV_C_W - 1, D_C)))
    scratch = list(_MixerScratch(
        bufa=_conv_buf(CONV_A_W, PAD_A, rows, n_t, D_A),
        bufb=_conv_buf(CONV_B_W, PAD_B, rows, n_t, D_XBC),
        bufc=_conv_buf(CONV_C_W, PAD_C, rows, n_t, D_C),
        st=pltpu.VMEM((rows, D_STATE, D_B), F32),
        xbc=pltpu.VMEM((m, D_XBC), F32),
        dt=pltpu.VMEM((m, HEAD_PAD), F32),
        ycat=pltpu.VMEM((m, D_MIX), BF16),
        h=pltpu.VMEM((m, D_MODEL), BF16),
        ab=pltpu.VMEM((m, D_A), F32),
        z=pltpu.VMEM((m, D_B), F32))) * n_chain
    return pl.pallas_call(
        functools.partial(_mixer_kernel, n_b=n_b, n_t=n_t, q=q, n_chain=n_chain),
        grid=grid, in_specs=in_specs, out_specs=out_specs, out_shape=out_shape,
        scratch_shapes=scratch,
        compiler_params=pltpu.CompilerParams(
            dimension_semantics=("arbitrary", "arbitrary"),
            vmem_limit_bytes=VMEM_LIMIT_BYTES),
        name="mixer",
    )(x, st_a, st_ssm, st_b, st_c, *consts)


class _FfnConsts(NamedTuple):
    gffn: object
    wu: object
    wg: object
    cwf: object
    wd: object
    gfin: object


class _FfnScratch(NamedTuple):
    buff: object
    hid: object


def _ffn_project(x_ref, w, s, b0, rows, n_t):
    m = rows * n_t
    x = x_ref[b0:b0 + rows].reshape(m, D_MODEL)
    h = (_rms_scale(x) * w.gffn[...]).astype(BF16)
    up = _dot(h, w.wu[...])
    _conv_stage(s.buff, _dot(h, w.wg[...]).reshape(rows, n_t, D_FF), CONV_F_W, PAD_F, n_t)
    return x, up


def _ffn_finish(x, up, xo_ref, nf_ref, w, s, b0, rows, n_t, final_norm):
    def emit_f(b, r, y):
        r0 = b * n_t + r
        s.hid[r0:r0 + y.shape[0], :] = (_silu(y) * up[r0:r0 + y.shape[0]]).astype(BF16)

    _run(_dwconv_items(s.buff, w.cwf, CONV_F_W, PAD_F, rows, n_t, SUBLANES, emit_f))
    nf_ref[b0:b0 + rows] = _carry_tail(s.buff, CONV_F_W, PAD_F, n_t)

    out = x + _dot(s.hid[...], w.wd[:, :D_MODEL])
    if final_norm:
        out = _rms_scale(out) * w.gfin[...]
    xo_ref[b0:b0 + rows] = out.reshape(rows, n_t, D_MODEL)


def _ffn_kernel(*refs, n_b, n_t, n_chain, final_norm):
    x_ref, stf_ref = refs[:2]
    n_in = 2 + len(_FfnConsts._fields)
    w = _FfnConsts(*refs[2:n_in])
    xo_ref, nf_ref = refs[n_in:n_in + 2]
    flat = refs[n_in + 2:]
    per = len(_FfnScratch._fields)
    chains = [_FfnScratch(*flat[c * per:(c + 1) * per]) for c in range(n_chain)]
    rows = n_b // n_chain

    @pl.when(pl.program_id(1) == 0)
    def _load_state():
        for c, s in enumerate(chains):
            _conv_load_tail(s.buff, stf_ref[c * rows:(c + 1) * rows], CONV_F_W, PAD_F)

    staged = [_ffn_project(x_ref, w, s, c * rows, rows, n_t) for c, s in enumerate(chains)]
    for c, s in enumerate(chains):
        x, up = staged[c]
        _ffn_finish(x, up, xo_ref, nf_ref, w, s, c * rows, rows, n_t, final_norm)


def _ffn_call(x, st_f, w, gfin, *, n_b, n_t, n_chain, final_norm):
    bsz, seq, _ = x.shape
    assert bsz % n_b == 0 and seq % n_t == 0 and n_t % SUBLANES == 0 and n_b % n_chain == 0
    rows = n_b // n_chain
    m = rows * n_t
    grid = (bsz // n_b, seq // n_t)
    consts = _FfnConsts(gfin=gfin, **{k: w[k] for k in _FfnConsts._fields if k != 'gfin'})
    in_specs = ([pl.BlockSpec((n_b, n_t, D_MODEL), lambda b, t: (b, t, 0)),
                 pl.BlockSpec((n_b, CONV_F_W - 1, D_FF), lambda b, t: (b, 0, 0))]
                + [_const_spec(c.shape) for c in consts])
    out_shape = (jax.ShapeDtypeStruct(x.shape, F32), jax.ShapeDtypeStruct(st_f.shape, F32))
    out_specs = (pl.BlockSpec((n_b, n_t, D_MODEL), lambda b, t: (b, t, 0)),
                 pl.BlockSpec((n_b, CONV_F_W - 1, D_FF), lambda b, t: (b, 0, 0)))
    scratch = list(_FfnScratch(buff=_conv_buf(CONV_F_W, PAD_F, rows, n_t, D_FF),
                               hid=pltpu.VMEM((m, D_FF), BF16))) * n_chain
    return pl.pallas_call(
        functools.partial(_ffn_kernel, n_b=n_b, n_t=n_t, n_chain=n_chain, final_norm=final_norm),
        grid=grid, in_specs=in_specs, out_specs=out_specs, out_shape=out_shape,
        scratch_shapes=scratch,
        compiler_params=pltpu.CompilerParams(
            dimension_semantics=("arbitrary", "arbitrary"),
            vmem_limit_bytes=VMEM_LIMIT_BYTES),
        name="ffn",
    )(x, st_f, *consts)


def _row(v, pad_to=None):
    v = v.astype(F32).reshape(1, -1)
    if pad_to is not None:
        v = jnp.pad(v, ((0, 0), (0, pad_to - v.shape[1])))
    return v


def _mxu_weight(w):
    w = w.astype(BF16)
    if w.shape[-1] % RELAYOUT_MINOR == 0:
        w = jnp.pad(w, ((0, 0), (0, LANES)))
    return w


def _taps(w):
    return jnp.broadcast_to(w.astype(F32)[:, None, :], (w.shape[0], SUBLANES, w.shape[1]))


def _layer_weights(i, norm_mix_g, w_in, conv_a_w, conv_b_w, conv_b_bias, dt_bias, a_log, d_skip,
                   ssm_norm_g, conv_c_w, conv_c_bias, ln_c_g, ln_c_b, w_out, norm_ffn_g, w_up,
                   conv_ffn_w, w_down):
    wi = w_in[i]
    c0 = 3 * D_A
    c1 = c0 + D_B
    c2 = c1 + D_XBC
    c3 = c2 + N_HEADS
    return dict(
        gmix=_row(norm_mix_g[i]),
        wa=_mxu_weight(wi[:, :c0]), wz=_mxu_weight(wi[:, c0:c1]),
        wxbc=_mxu_weight(wi[:, c1:c2]),
        wdt=jnp.pad(wi[:, c2:c3], ((0, 0), (0, HEAD_PAD - N_HEADS))).astype(BF16),
        wc=_mxu_weight(wi[:, c3:]),
        cwa=_taps(conv_a_w[i]), cwb=_taps(conv_b_w[i]), cbb=_row(conv_b_bias[i]),
        dtb=_row(dt_bias[i], HEAD_PAD), alog=_row(a_log[i], HEAD_PAD),
        dskip=_row(jnp.repeat(d_skip[i], HEAD_DIM)), gssm=_row(ssm_norm_g[i]),
        cwc=_taps(conv_c_w[i]), cbc=_row(conv_c_bias[i]), lng=_row(ln_c_g[i]), lnb=_row(ln_c_b[i]),
        wout=_mxu_weight(w_out[i]),
        gffn=_row(norm_ffn_g[i]),
        wu=_mxu_weight(w_up[i][:, :D_FF]), wg=_mxu_weight(w_up[i][:, D_FF:]),
        cwf=_taps(conv_ffn_w[i]), wd=_mxu_weight(w_down[i]),
    )


class _Tile(NamedTuple):
    rows: int
    tokens: int
    chains: int


class _StreamTiles(NamedTuple):
    mixer: _Tile
    ffn: _Tile


PROMPT_TILES = _StreamTiles(mixer=_Tile(1, 256, 1), ffn=_Tile(2, 128, 2))
SAMPLE_TILES = _StreamTiles(mixer=_Tile(4, 32, 1), ffn=_Tile(8, 32, 1))


def _layer(x, st_a, st_ssm, st_b, st_c, st_f, w, gfin, *, tiles, final_norm):
    bsz = x.shape[0]
    tm, tf = tiles.mixer, tiles.ffn
    x, na, ns, nb, nc = _mixer_call(x, st_a, st_ssm.reshape(bsz, D_B, D_STATE), st_b, st_c, w,
                                    n_b=tm.rows, n_t=tm.tokens, n_chain=tm.chains)
    x, nf = _ffn_call(x, st_f, w, gfin, n_b=tf.rows, n_t=tf.tokens, n_chain=tf.chains,
                      final_norm=final_norm)
    return x, (na, ns.reshape(bsz, N_HEADS, HEAD_DIM, D_STATE), nb, nc, nf)


def kernel(x_prompt, x_sample, state_conv_a, state_ssm, state_conv_b, state_conv_c, state_conv_ffn, norm_mix_g, w_in, conv_a_w, conv_b_w, conv_b_bias, dt_bias, a_log, d_skip, ssm_norm_g, conv_c_w, conv_c_bias, ln_c_g, ln_c_b, w_out, norm_ffn_g, w_up, conv_ffn_w, w_down, final_norm_g):
    depth = w_in.shape[0]
    bp = x_prompt.shape[0]
    zeros = (jnp.zeros((bp, CONV_A_W - 1, D_A), F32),
             jnp.zeros((bp, N_HEADS, HEAD_DIM, D_STATE), F32),
             jnp.zeros((bp, CONV_B_W - 1, D_XBC), F32),
             jnp.zeros((bp, CONV_C_W - 1, D_C), F32),
             jnp.zeros((bp, CONV_F_W - 1, D_FF), F32))
    gfin = _row(final_norm_g)
    hp, hs = x_prompt, x_sample
    p_states, s_states = [], []
    for i in range(depth):
        w = _layer_weights(i, norm_mix_g, w_in, conv_a_w, conv_b_w, conv_b_bias, dt_bias, a_log,
                           d_skip, ssm_norm_g, conv_c_w, conv_c_bias, ln_c_g, ln_c_b, w_out,
                           norm_ffn_g, w_up, conv_ffn_w, w_down)
        last = i == depth - 1
        hp, st = _layer(hp, *zeros, w, gfin, tiles=PROMPT_TILES, final_norm=last)
        p_states.append(st)
        hs, st = _layer(hs, state_conv_a[i], state_ssm[i], state_conv_b[i], state_conv_c[i],
                        state_conv_ffn[i], w, gfin, tiles=SAMPLE_TILES, final_norm=last)
        s_states.append(st)
    p_out = [jnp.stack([s[j] for s in p_states]) for j in range(5)]
    s_out = [jnp.stack([s[j] for s in s_states]) for j in range(5)]
    return (hp, hs, *p_out, *s_out)
```

```python
import functools
from typing import NamedTuple

import jax
import jax.numpy as jnp
from jax import lax
from jax.experimental import pallas as pl
from jax.experimental.pallas import tpu as pltpu

F32 = jnp.float32
BF16 = jnp.bfloat16

D_MODEL = 1024
D_A = 512
D_B = 1024
D_C = 512
N_HEADS = 16
HEAD_DIM = 64
N_GROUPS = 2
HEADS_PER_GROUP = N_HEADS // N_GROUPS
GROUP_W = HEADS_PER_GROUP * HEAD_DIM
D_STATE = 128
D_XBC = D_B + 2 * N_GROUPS * D_STATE
D_FF = 2816
D_MIX = D_A + D_B + D_C
CONV_A_W, CONV_B_W, CONV_C_W, CONV_F_W = 3, 4, 31, 3
EPS = 1e-5
SSD_CHUNK = 64

LANES = 128
SUBLANES = 8
MXU_DIM = 256
HEAD_PAD = LANES
RELAYOUT_MINOR = 1024
VMEM_LIMIT_BYTES = 56 * 1024 * 1024

PAD_A, PAD_B, PAD_C, PAD_F = 8, 8, 32, 8


def _dot(a, b):
    return jnp.dot(a, b, preferred_element_type=F32)


def _dot_nt(a, b):
    return lax.dot_general(a, b, (((1,), (1,)), ((), ())), preferred_element_type=F32)


def _split2(x):
    hi = x.astype(BF16)
    return hi, (x - hi.astype(F32)).astype(BF16)


def _split3(x):
    hi = x.astype(BF16)
    r1 = x - hi.astype(F32)
    mid = r1.astype(BF16)
    return hi, mid, (r1 - mid.astype(F32)).astype(BF16)


def _silu(x):
    return x * jax.nn.sigmoid(x)


def _rms_scale(x):
    return x * lax.rsqrt(jnp.mean(x * x, axis=-1, keepdims=True) + EPS)


def _n_shifts(k):
    return min(k, SUBLANES)


def _conv_buf(k, pad, n_b, n_t, c):
    return pltpu.VMEM((_n_shifts(k), n_b, pad + n_t + SUBLANES, c), F32)


def _conv_load_tail(buf_ref, st, k, pad):
    for v in range(_n_shifts(k)):
        buf_ref[v, :, pad - (k - 1) + v:pad + v, :] = st


def _conv_stage(buf_ref, val, k, pad, n_t):
    for v in range(_n_shifts(k)):
        buf_ref[v, :, pad + v:pad + v + n_t, :] = val


def _dwconv_items(buf_ref, w_ref, k, pad, n_b, n_t, rows, emit):
    c = buf_ref.shape[-1]

    def block(b, r):
        acc = None
        for j in range(k):
            s = pad - (k - 1) + j + r
            v = (-s) % SUBLANES
            win = buf_ref[v, b, s + v:s + v + rows, :].reshape(rows // SUBLANES, SUBLANES, c)
            term = win * w_ref[j]
            acc = term if acc is None else acc + term
        emit(b, r, acc.reshape(rows, c))

    return [functools.partial(block, b, r) for b in range(n_b) for r in range(0, n_t, rows)]


def _run(items):
    for item in items:
        item()


def _carry_tail(buf_ref, k, pad, n_t):
    tail = buf_ref[0, :, pad + n_t - (k - 1):pad + n_t, :]
    for v in range(_n_shifts(k)):
        buf_ref[v, :, pad - (k - 1) + v:pad + v, :] = buf_ref[v, :, pad + n_t - (k - 1) + v:pad + n_t + v, :]
    return tail


class _MixerConsts(NamedTuple):
    gmix: object
    wa: object
    wz: object
    wxbc: object
    wc: object
    wdt: object
    cwa: object
    cwb: object
    cbb: object
    dtb: object
    alog: object
    dskip: object
    gssm: object
    cwc: object
    cbc: object
    lng: object
    lnb: object
    wout: object
    e2p: object
    e2q: object


class _MixerScratch(NamedTuple):
    bufa: object
    bufb: object
    bufc: object
    st: object
    xbc: object
    dt: object
    ycat: object
    h: object
    ab: object
    z: object


class _SsdMasks(NamedTuple):
    tril: object
    ones2: object
    causal_rep: object
    eye_rep: object
    blockdiag: object


def _ssd_masks(q):
    hq = N_HEADS * q
    row = lax.broadcasted_iota(jnp.int32, (q, q), 0)
    col = lax.broadcasted_iota(jnp.int32, (q, q), 1)
    rowq = lax.broadcasted_iota(jnp.int32, (q, hq), 0)
    colq = lax.broadcasted_iota(jnp.int32, (q, hq), 1) % q
    heads_per_blk = MXU_DIM // q
    blk_w = heads_per_blk * HEAD_DIM
    bd_r = lax.broadcasted_iota(jnp.int32, (MXU_DIM, blk_w), 0) // q
    bd_c = lax.broadcasted_iota(jnp.int32, (MXU_DIM, blk_w), 1) // HEAD_DIM
    return _SsdMasks(tril=(col <= row).astype(BF16),
                     ones2=jnp.ones((q, 2 * q), BF16),
                     causal_rep=colq <= rowq,
                     eye_rep=(colq == rowq).astype(F32),
                     blockdiag=bd_r == bd_c)


def _group(x, g, width):
    return x[:, g * width:(g + 1) * width]


class _SsdScan:
    def __init__(self, w, s, masks, rows, n_t, q):
        self.w, self.s, self.masks = w, s, masks
        self.rows, self.q = rows, q
        self.m = rows * n_t
        self.chunks_per_row = n_t // q
        self.chunk_rows = [slice(c * q, (c + 1) * q) for c in range(self.m // q)]

    def cumsum(self):
        w, s, masks, m, q = self.w, self.s, self.masks, self.m, self.q
        a_row = -jnp.exp(w.alog[...])
        dt = s.dt[...]
        dta = dt * a_row

        def chunk_cumsum(rc):
            acs = _dot(masks.tril, jnp.concatenate(_split3(dta[rc]), axis=-1))
            return acs[:, :HEAD_PAD] + acs[:, HEAD_PAD:2 * HEAD_PAD] + acs[:, 2 * HEAD_PAD:]

        a_cum = jnp.concatenate([chunk_cumsum(rc) for rc in self.chunk_rows], axis=0)
        a_pieces = jnp.concatenate(_split2(a_cum), axis=-1)
        ex = _dot(jnp.concatenate([a_pieces, jnp.concatenate(_split2(dt), axis=-1)], axis=0),
                  w.e2p[:, :D_B])
        self.a_x, self.dt_x = ex[:m], ex[m:]
        self.a_q = self.a_x if q == HEAD_DIM else _dot(a_pieces, w.e2q[:, :N_HEADS * q])

    def decay(self):
        masks = self.masks

        def chunk_decay(rc):
            a_q = self.a_q[rc]
            a_row_b = _dot(masks.ones2, jnp.concatenate(_split2(a_q * masks.eye_rep), axis=0))
            return jnp.exp(jnp.where(masks.causal_rep, a_q - a_row_b, -jnp.inf))

        self.decays = [chunk_decay(rc) for rc in self.chunk_rows]

    def scores(self):
        xbc = self.s.xbc[...]
        self.xs = xbc[:, :D_B]
        self.bm = xbc[:, D_B:D_B + N_GROUPS * D_STATE]
        self.cm_b = xbc[:, D_B + N_GROUPS * D_STATE:].astype(BF16)
        bm_b = self.bm.astype(BF16)
        self.xdt = self.xs * self.dt_x
        self.xdt_b = self.xdt.astype(BF16)

        def chunk_scores(c, rc):
            cbt = jnp.concatenate(
                [_dot_nt(_group(self.cm_b[rc], g, D_STATE),
                         jnp.concatenate([_group(bm_b[rc], g, D_STATE)] * HEADS_PER_GROUP, axis=0))
                 for g in range(N_GROUPS)], axis=-1)
            return (cbt * self.decays[c]).astype(BF16)

        self.m_all = [chunk_scores(c, rc) for c, rc in enumerate(self.chunk_rows)]

    def within_chunk(self):
        heads_per_blk = MXU_DIM // self.q
        blk_w = heads_per_blk * HEAD_DIM

        def chunk_y(c, rc):
            parts = []
            for blk in range(N_HEADS // heads_per_blk):
                xb = _group(self.xdt_b[rc], blk, blk_w)
                rhs = jnp.where(self.masks.blockdiag, jnp.concatenate([xb] * heads_per_blk, axis=0), 0)
                parts.append(_dot(_group(self.m_all[c], blk, MXU_DIM), rhs))
            return jnp.concatenate(parts, axis=-1)

        self.y_diag = jnp.concatenate([chunk_y(c, rc) for c, rc in enumerate(self.chunk_rows)], axis=0)
        self.a_last = [self.a_x[rc.stop - 1:rc.stop, :] for rc in self.chunk_rows]

        def chunk_update(c, rc):
            xw_b = (self.xdt[rc] * jnp.exp(self.a_last[c] - self.a_x[rc])).astype(BF16)
            return jnp.concatenate(
                [_dot(_group(self.bm[rc], g, D_STATE).T.astype(BF16), _group(xw_b, g, GROUP_W))
                 for g in range(N_GROUPS)], axis=-1)

        self.upd = [chunk_update(c, rc) for c, rc in enumerate(self.chunk_rows)]

    def recur(self):
        w, s = self.w, self.s
        y_off = []
        for b in range(self.rows):
            st = s.st[b]
            for j in range(self.chunks_per_row):
                c = b * self.chunks_per_row + j
                st_b = st.astype(BF16)
                y_off.append(jnp.concatenate(
                    [_dot(_group(self.cm_b[self.chunk_rows[c]], g, D_STATE), _group(st_b, g, GROUP_W))
                     for g in range(N_GROUPS)], axis=-1))
                st = st * jnp.exp(self.a_last[c]) + self.upd[c]
            s.st[b] = st
        y = self.y_diag + jnp.concatenate(y_off, axis=0) * jnp.exp(self.a_x) + w.dskip[...] * self.xs
        yg = y * _silu(s.z[...])
        s.ycat[:, D_A:D_A + D_B] = (_rms_scale(yg) * w.gssm[...]).astype(BF16)


class _MixerChain:
    def __init__(self, x_ref, xo_ref, na_ref, nb_ref, nc_ref, w, s, masks, b0, rows, n_t, q):
        self.x_ref, self.xo_ref = x_ref, xo_ref
        self.na_ref, self.nb_ref, self.nc_ref = na_ref, nb_ref, nc_ref
        self.w, self.s, self.masks = w, s, masks
        self.b0, self.rows, self.n_t, self.q = b0, rows, n_t, q

    def norm(self):
        x = self.x_ref[self.b0:self.b0 + self.rows].reshape(self.rows * self.n_t, D_MODEL)
        self.s.h[...] = (_rms_scale(x) * self.w.gmix[...]).astype(BF16)

    def project_items(self):
        w, s, rows, n_t = self.w, self.s, self.rows, self.n_t

        def xbc():
            _conv_stage(s.bufb, _dot(s.h[...], w.wxbc[...]).reshape(rows, n_t, D_XBC), CONV_B_W, PAD_B, n_t)
            s.dt[...] = jax.nn.softplus(_dot(s.h[...], w.wdt[...]) + w.dtb[...])

        def a():
            pa = _dot(s.h[...], w.wa[...])
            s.ab[...] = pa[:, D_A:2 * D_A]
            _conv_stage(s.bufa, (pa[:, 2 * D_A:] * pa[:, :D_A]).reshape(rows, n_t, D_A),
                        CONV_A_W, PAD_A, n_t)

        def c():
            pc = _dot(s.h[...], w.wc[:, :2 * D_C])
            _conv_stage(s.bufc, (pc[:, :D_C] * jax.nn.sigmoid(pc[:, D_C:])).reshape(rows, n_t, D_C),
                        CONV_C_W, PAD_C, n_t)

        def z():
            s.z[...] = _dot(s.h[...], w.wz[:, :D_B])

        return dict(xbc=xbc, a=a, c=c, z=z)

    def conv_items(self):
        w, s, b0, rows, n_t = self.w, self.s, self.b0, self.rows, self.n_t

        def emit_b(b, r, y):
            r0 = b * n_t + r
            s.xbc[r0:r0 + y.shape[0], :] = _silu(y + w.cbb[...])

        def emit_a(b, r, y):
            r0 = b * n_t + r
            s.ycat[r0:r0 + y.shape[0], 0:D_A] = (s.ab[r0:r0 + y.shape[0], :] * y).astype(BF16)

        def emit_c(b, r, y):
            r0 = b * n_t + r
            y = y + w.cbc[...]
            mu = jnp.mean(y, axis=-1, keepdims=True)
            yc = y - mu
            var = jnp.mean(yc * yc, axis=-1, keepdims=True)
            yn = yc * lax.rsqrt(var + EPS) * w.lng[...] + w.lnb[...]
            s.ycat[r0:r0 + y.shape[0], D_A + D_B:D_MIX] = _silu(yn).astype(BF16)

        def tail_b():
            self.nb_ref[b0:b0 + rows] = _carry_tail(s.bufb, CONV_B_W, PAD_B, n_t)

        def tail_a():
            self.na_ref[b0:b0 + rows] = _carry_tail(s.bufa, CONV_A_W, PAD_A, n_t)

        def tail_c():
            self.nc_ref[b0:b0 + rows] = _carry_tail(s.bufc, CONV_C_W, PAD_C, n_t)

        return dict(
            b=_dwconv_items(s.bufb, w.cwb, CONV_B_W, PAD_B, rows, n_t, min(n_t, 16), emit_b) + [tail_b],
            a=_dwconv_items(s.bufa, w.cwa, CONV_A_W, PAD_A, rows, n_t, min(n_t, 32), emit_a) + [tail_a],
            c=_dwconv_items(s.bufc, w.cwc, CONV_C_W, PAD_C, rows, n_t, min(n_t, 32), emit_c) + [tail_c])

    def out(self):
        s, n_t = self.s, self.n_t
        y = _dot(s.ycat[...], self.w.wout[:, :D_MODEL]).reshape(self.rows, n_t, D_MODEL)
        self.xo_ref[self.b0:self.b0 + self.rows] = self.x_ref[self.b0:self.b0 + self.rows] + y

    def run(self):
        proj, conv = self.project_items(), self.conv_items()
        scan = _SsdScan(self.w, self.s, self.masks, self.rows, self.n_t, self.q)
        self.norm()
        proj['xbc']()
        _run(conv['b'])
        proj['a']()
        scan.cumsum()
        scan.decay()
        proj['c']()
        scan.scores()
        proj['z']()
        scan.within_chunk()
        _run(conv['a'] + conv['c'])
        scan.recur()
        self.out()


N_MIXER_STATE_IN = 4


def _mixer_kernel(*refs, n_b, n_t, q, n_chain):
    x_ref, sta_ref, sts_ref, stb_ref, stc_ref = refs[:1 + N_MIXER_STATE_IN]
    n_in = 1 + N_MIXER_STATE_IN + len(_MixerConsts._fields)
    w = _MixerConsts(*refs[1 + N_MIXER_STATE_IN:n_in])
    xo_ref, na_ref, ns_ref, nb_ref, nc_ref = refs[n_in:n_in + 5]
    flat = refs[n_in + 5:]
    per = len(_MixerScratch._fields)
    chains = [_MixerScratch(*flat[c * per:(c + 1) * per]) for c in range(n_chain)]
    rows = n_b // n_chain
    t = pl.program_id(1)

    @pl.when(t == 0)
    def _load_state():
        for c, s in enumerate(chains):
            b0 = c * rows
            _conv_load_tail(s.bufa, sta_ref[b0:b0 + rows], CONV_A_W, PAD_A)
            _conv_load_tail(s.bufb, stb_ref[b0:b0 + rows], CONV_B_W, PAD_B)
            _conv_load_tail(s.bufc, stc_ref[b0:b0 + rows], CONV_C_W, PAD_C)
            for b in range(rows):
                s.st[b] = sts_ref[b0 + b].T

    masks = _ssd_masks(q)
    cs = [_MixerChain(x_ref, xo_ref, na_ref, nb_ref, nc_ref, w, s, masks, c * rows, rows, n_t, q)
          for c, s in enumerate(chains)]
    for chain in cs:
        chain.run()

    @pl.when(t == pl.num_programs(1) - 1)
    def _store_state():
        for c, s in enumerate(chains):
            for b in range(rows):
                ns_ref[c * rows + b] = s.st[b].T


def _expand_matrix(width):
    head = jnp.arange(HEAD_PAD)[:, None]
    lane_head = (jnp.arange(N_HEADS * width) // width)[None, :]
    e = (head == lane_head).astype(F32)
    return _mxu_weight(jnp.concatenate([e, e], axis=0))


def _const_spec(shape):
    nd = len(shape)
    return pl.BlockSpec(shape, lambda b, t: (0,) * nd, pipeline_mode=pl.Buffered(1))


def _mixer_call(x, st_a, st_ssm, st_b, st_c, w, *, n_b, n_t, n_chain):
    bsz, seq, _ = x.shape
    q = min(SSD_CHUNK, n_t)
    assert bsz % n_b == 0 and seq % n_t == 0 and n_t % q == 0 and MXU_DIM % q == 0
    assert n_b % n_chain == 0 and n_t >= CONV_C_W - 1 and n_t % SUBLANES == 0
    rows = n_b // n_chain
    m = rows * n_t
    grid = (bsz // n_b, seq // n_t)

    def bspec(shape):
        return pl.BlockSpec((n_b,) + shape, lambda b, t: (b, 0, 0))

    consts = _MixerConsts(e2p=_expand_matrix(HEAD_DIM), e2q=_expand_matrix(q),
                          **{k: w[k] for k in _MixerConsts._fields if k not in ('e2p', 'e2q')})
    in_specs = ([pl.BlockSpec((n_b, n_t, D_MODEL), lambda b, t: (b, t, 0)),
                 bspec((CONV_A_W - 1, D_A)), bspec((D_B, D_STATE)),
                 bspec((CONV_B_W - 1, D_XBC)), bspec((CONV_C_W - 1, D_C))]
                + [_const_spec(c.shape) for c in consts])
    out_shape = (jax.ShapeDtypeStruct(x.shape, F32),
                 jax.ShapeDtypeStruct(st_a.shape, F32),
                 jax.ShapeDtypeStruct(st_ssm.shape, F32),
                 jax.ShapeDtypeStruct(st_b.shape, F32),
                 jax.ShapeDtypeStruct(st_c.shape, F32))
    out_specs = (pl.BlockSpec((n_b, n_t, D_MODEL), lambda b, t: (b, t, 0)),
                 bspec((CONV_A_W - 1, D_A)), bspec((D_B, D_STATE)),
                 bspec((CONV_B_W - 1, D_XBC)), bspec((CONV_C_W - 1, D_C)))
    scratch = list(_MixerScratch(
        bufa=_conv_buf(CONV_A_W, PAD_A, rows, n_t, D_A),
        bufb=_conv_buf(CONV_B_W, PAD_B, rows, n_t, D_XBC),
        bufc=_conv_buf(CONV_C_W, PAD_C, rows, n_t, D_C),
        st=pltpu.VMEM((rows, D_STATE, D_B), F32),
        xbc=pltpu.VMEM((m, D_XBC), F32),
        dt=pltpu.VMEM((m, HEAD_PAD), F32),
        ycat=pltpu.VMEM((m, D_MIX), BF16),
        h=pltpu.VMEM((m, D_MODEL), BF16),
        ab=pltpu.VMEM((m, D_A), F32),
        z=pltpu.VMEM((m, D_B), F32))) * n_chain
    return pl.pallas_call(
        functools.partial(_mixer_kernel, n_b=n_b, n_t=n_t, q=q, n_chain=n_chain),
        grid=grid, in_specs=in_specs, out_specs=out_specs, out_shape=out_shape,
        scratch_shapes=scratch,
        compiler_params=pltpu.CompilerParams(
            dimension_semantics=("arbitrary", "arbitrary"),
            vmem_limit_bytes=VMEM_LIMIT_BYTES),
        name="mixer",
    )(x, st_a, st_ssm, st_b, st_c, *consts)


class _FfnConsts(NamedTuple):
    gffn: object
    wu: object
    wg: object
    cwf: object
    wd: object
    gfin: object


class _FfnScratch(NamedTuple):
    buff: object
    hid: object


def _ffn_project(x_ref, w, s, b0, rows, n_t):
    m = rows * n_t
    x = x_ref[b0:b0 + rows].reshape(m, D_MODEL)
    h = (_rms_scale(x) * w.gffn[...]).astype(BF16)
    up = _dot(h, w.wu[...])
    _conv_stage(s.buff, _dot(h, w.wg[...]).reshape(rows, n_t, D_FF), CONV_F_W, PAD_F, n_t)
    return x, up


def _ffn_finish(x, up, xo_ref, nf_ref, w, s, b0, rows, n_t, final_norm):
    def emit_f(b, r, y):
        r0 = b * n_t + r
        s.hid[r0:r0 + y.shape[0], :] = (_silu(y) * up[r0:r0 + y.shape[0]]).astype(BF16)

    _run(_dwconv_items(s.buff, w.cwf, CONV_F_W, PAD_F, rows, n_t, SUBLANES, emit_f))
    nf_ref[b0:b0 + rows] = _carry_tail(s.buff, CONV_F_W, PAD_F, n_t)

    out = x + _dot(s.hid[...], w.wd[:, :D_MODEL])
    if final_norm:
        out = _rms_scale(out) * w.gfin[...]
    xo_ref[b0:b0 + rows] = out.reshape(rows, n_t, D_MODEL)


def _ffn_kernel(*refs, n_b, n_t, n_chain, final_norm):
    x_ref, stf_ref = refs[:2]
    n_in = 2 + len(_FfnConsts._fields)
    w = _FfnConsts(*refs[2:n_in])
    xo_ref, nf_ref = refs[n_in:n_in + 2]
    flat = refs[n_in + 2:]
    per = len(_FfnScratch._fields)
    chains = [_FfnScratch(*flat[c * per:(c + 1) * per]) for c in range(n_chain)]
    rows = n_b // n_chain

    @pl.when(pl.program_id(1) == 0)
    def _load_state():
        for c, s in enumerate(chains):
            _conv_load_tail(s.buff, stf_ref[c * rows:(c + 1) * rows], CONV_F_W, PAD_F)

    staged = [_ffn_project(x_ref, w, s, c * rows, rows, n_t) for c, s in enumerate(chains)]
    for c, s in enumerate(chains):
        x, up = staged[c]
        _ffn_finish(x, up, xo_ref, nf_ref, w, s, c * rows, rows, n_t, final_norm)


def _ffn_call(x, st_f, w, gfin, *, n_b, n_t, n_chain, final_norm):
    bsz, seq, _ = x.shape
    assert bsz % n_b == 0 and seq % n_t == 0 and n_t % SUBLANES == 0 and n_b % n_chain == 0
    rows = n_b // n_chain
    m = rows * n_t
    grid = (bsz // n_b, seq // n_t)
    consts = _FfnConsts(gfin=gfin, **{k: w[k] for k in _FfnConsts._fields if k != 'gfin'})
    in_specs = ([pl.BlockSpec((n_b, n_t, D_MODEL), lambda b, t: (b, t, 0)),
                 pl.BlockSpec((n_b, CONV_F_W - 1, D_FF), lambda b, t: (b, 0, 0))]
                + [_const_spec(c.shape) for c in consts])
    out_shape = (jax.ShapeDtypeStruct(x.shape, F32), jax.ShapeDtypeStruct(st_f.shape, F32))
    out_specs = (pl.BlockSpec((n_b, n_t, D_MODEL), lambda b, t: (b, t, 0)),
                 pl.BlockSpec((n_b, CONV_F_W - 1, D_FF), lambda b, t: (b, 0, 0)))
    scratch = list(_FfnScratch(buff=_conv_buf(CONV_F_W, PAD_F, rows, n_t, D_FF),
                               hid=pltpu.VMEM((m, D_FF), BF16))) * n_chain
    return pl.pallas_call(
        functools.partial(_ffn_kernel, n_b=n_b, n_t=n_t, n_chain=n_chain, final_norm=final_norm),
        grid=grid, in_specs=in_specs, out_specs=out_specs, out_shape=out_shape,
        scratch_shapes=scratch,
        compiler_params=pltpu.CompilerParams(
            dimension_semantics=("arbitrary", "arbitrary"),
            vmem_limit_bytes=VMEM_LIMIT_BYTES),
        name="ffn",
    )(x, st_f, *consts)


def _row(v, pad_to=None):
    v = v.astype(F32).reshape(1, -1)
    if pad_to is not None:
        v = jnp.pad(v, ((0, 0), (0, pad_to - v.shape[1])))
    return v


def _mxu_weight(w):
    w = w.astype(BF16)
    if w.shape[-1] % RELAYOUT_MINOR == 0:
        w = jnp.pad(w, ((0, 0), (0, LANES)))
    return w


def _taps(w):
    return jnp.broadcast_to(w.astype(F32)[:, None, :], (w.shape[0], SUBLANES, w.shape[1]))


def _layer_weights(i, norm_mix_g, w_in, conv_a_w, conv_b_w, conv_b_bias, dt_bias, a_log, d_skip,
                   ssm_norm_g, conv_c_w, conv_c_bias, ln_c_g, ln_c_b, w_out, norm_ffn_g, w_up,
                   conv_ffn_w, w_down):
    wi = w_in[i]
    c0 = 3 * D_A
    c1 = c0 + D_B
    c2 = c1 + D_XBC
    c3 = c2 + N_HEADS
    return dict(
        gmix=_row(norm_mix_g[i]),
        wa=_mxu_weight(wi[:, :c0]), wz=_mxu_weight(wi[:, c0:c1]),
        wxbc=_mxu_weight(wi[:, c1:c2]),
        wdt=jnp.pad(wi[:, c2:c3], ((0, 0), (0, HEAD_PAD - N_HEADS))).astype(BF16),
        wc=_mxu_weight(wi[:, c3:]),
        cwa=_taps(conv_a_w[i]), cwb=_taps(conv_b_w[i]), cbb=_row(conv_b_bias[i]),
        dtb=_row(dt_bias[i], HEAD_PAD), alog=_row(a_log[i], HEAD_PAD),
        dskip=_row(jnp.repeat(d_skip[i], HEAD_DIM)), gssm=_row(ssm_norm_g[i]),
        cwc=_taps(conv_c_w[i]), cbc=_row(conv_c_bias[i]), lng=_row(ln_c_g[i]), lnb=_row(ln_c_b[i]),
        wout=_mxu_weight(w_out[i]),
        gffn=_row(norm_ffn_g[i]),
        wu=_mxu_weight(w_up[i][:, :D_FF]), wg=_mxu_weight(w_up[i][:, D_FF:]),
        cwf=_taps(conv_ffn_w[i]), wd=_mxu_weight(w_down[i]),
    )


class _Tile(NamedTuple):
    rows: int
    tokens: int
    chains: int


class _StreamTiles(NamedTuple):
    mixer: _Tile
    ffn: _Tile


PROMPT_TILES = _StreamTiles(mixer=_Tile(1, 256, 1), ffn=_Tile(2, 256, 2))
SAMPLE_TILES = _StreamTiles(mixer=_Tile(4, 32, 1), ffn=_Tile(8, 32, 1))


def _layer(x, st_a, st_ssm, st_b, st_c, st_f, w, gfin, *, tiles, final_norm):
    bsz = x.shape[0]
    tm, tf = tiles.mixer, tiles.ffn
    x, na, ns, nb, nc = _mixer_call(x, st_a, st_ssm.reshape(bsz, D_B, D_STATE), st_b, st_c, w,
                                    n_b=tm.rows, n_t=tm.tokens, n_chain=tm.chains)
    x, nf = _ffn_call(x, st_f, w, gfin, n_b=tf.rows, n_t=tf.tokens, n_chain=tf.chains,
                      final_norm=final_norm)
    return x, (na, ns.reshape(bsz, N_HEADS, HEAD_DIM, D_STATE), nb, nc, nf)


def kernel(x_prompt, x_sample, state_conv_a, state_ssm, state_conv_b, state_conv_c, state_conv_ffn, norm_mix_g, w_in, conv_a_w, conv_b_w, conv_b_bias, dt_bias, a_log, d_skip, ssm_norm_g, conv_c_w, conv_c_bias, ln_c_g, ln_c_b, w_out, norm_ffn_g, w_up, conv_ffn_w, w_down, final_norm_g):
    depth = w_in.shape[0]
    bp = x_prompt.shape[0]
    zeros = (jnp.zeros((bp, CONV_A_W - 1, D_A), F32),
             jnp.zeros((bp, N_HEADS, HEAD_DIM, D_STATE), F32),
             jnp.zeros((bp, CONV_B_W - 1, D_XBC), F32),
             jnp.zeros((bp, CONV_C_W - 1, D_C), F32),
             jnp.zeros((bp, CONV_F_W - 1, D_FF), F32))
    gfin = _row(final_norm_g)
    hp, hs = x_prompt, x_sample
    p_states, s_states = [], []
    for i in range(depth):
        w = _layer_weights(i, norm_mix_g, w_in, conv_a_w, conv_b_w, conv_b_bias, dt_bias, a_log,
                           d_skip, ssm_norm_g, conv_c_w, conv_c_bias, ln_c_g, ln_c_b, w_out,
                           norm_ffn_g, w_up, conv_ffn_w, w_down)
        last = i == depth - 1
        hp, st = _layer(hp, *zeros, w, gfin, tiles=PROMPT_TILES, final_norm=last)
        p_states.append(st)
        hs, st = _layer(hs, state_conv_a[i], state_ssm[i], state_conv_b[i], state_conv_c[i],
                        state_conv_ffn[i], w, gfin, tiles=SAMPLE_TILES, final_norm=last)
        s_states.append(st)
    p_out = [jnp.stack([s[j] for s in p_states]) for j in range(5)]
    s_out = [jnp.stack([s[j] for s in s_states]) for j in range(5)]
    return (hp, hs, *p_out, *s_out)
```

```python
import functools
from typing import NamedTuple

import jax
import jax.numpy as jnp
from jax import lax
from jax.experimental import pallas as pl
from jax.experimental.pallas import tpu as pltpu

F32 = jnp.float32
BF16 = jnp.bfloat16

D_MODEL = 1024
D_A = 512
D_B = 1024
D_C = 512
N_HEADS = 16
HEAD_DIM = 64
N_GROUPS = 2
HEADS_PER_GROUP = N_HEADS // N_GROUPS
GROUP_W = HEADS_PER_GROUP * HEAD_DIM
D_STATE = 128
D_XBC = D_B + 2 * N_GROUPS * D_STATE
D_FF = 2816
D_MIX = D_A + D_B + D_C
CONV_A_W, CONV_B_W, CONV_C_W, CONV_F_W = 3, 4, 31, 3
EPS = 1e-5
SSD_CHUNK = 64

LANES = 128
SUBLANES = 8
MXU_DIM = 256
HEAD_PAD = LANES
RELAYOUT_MINOR = 1024
VMEM_LIMIT_BYTES = 56 * 1024 * 1024

PAD_A, PAD_B, PAD_C, PAD_F = 8, 8, 32, 8
ROWS_A, ROWS_B, ROWS_C, ROWS_F = 32, 16, 32, 8


def _dot(a, b):
    return jnp.dot(a, b, preferred_element_type=F32)


def _dot_nt(a, b):
    return lax.dot_general(a, b, (((1,), (1,)), ((), ())), preferred_element_type=F32)


def _split2(x):
    hi = x.astype(BF16)
    return hi, (x - hi.astype(F32)).astype(BF16)


def _split3(x):
    hi = x.astype(BF16)
    r1 = x - hi.astype(F32)
    mid = r1.astype(BF16)
    return hi, mid, (r1 - mid.astype(F32)).astype(BF16)


def _silu(x):
    return x * jax.nn.sigmoid(x)


def _rms_scale(x):
    return x * lax.rsqrt(jnp.mean(x * x, axis=-1, keepdims=True) + EPS)


def _n_shifts(k):
    return min(k, SUBLANES)


def _conv_buf(k, pad, n_b, n_t, c):
    return pltpu.VMEM((_n_shifts(k), n_b, pad + n_t + SUBLANES, c), F32)


def _conv_load_tail(buf_ref, st, k, pad):
    for v in range(_n_shifts(k)):
        buf_ref[v, :, pad - (k - 1) + v:pad + v, :] = st


def _conv_stage(buf_ref, val, k, pad, n_t):
    for v in range(_n_shifts(k)):
        buf_ref[v, :, pad + v:pad + v + n_t, :] = val


def _dwconv_items(buf_ref, w_ref, k, pad, n_b, n_t, rows, emit):
    c = buf_ref.shape[-1]

    def block(b, r):
        acc = None
        for j in range(k):
            s = pad - (k - 1) + j + r
            v = (-s) % SUBLANES
            win = buf_ref[v, b, s + v:s + v + rows, :].reshape(rows // SUBLANES, SUBLANES, c)
            term = win * w_ref[j]
            acc = term if acc is None else acc + term
        emit(b, r, acc.reshape(rows, c))

    return [functools.partial(block, b, r) for b in range(n_b) for r in range(0, n_t, rows)]


def _run(items):
    for item in items:
        item()


def _carry_tail(buf_ref, k, pad, n_t):
    tail = buf_ref[0, :, pad + n_t - (k - 1):pad + n_t, :]
    for v in range(_n_shifts(k)):
        buf_ref[v, :, pad - (k - 1) + v:pad + v, :] = buf_ref[v, :, pad + n_t - (k - 1) + v:pad + n_t + v, :]
    return tail


class _MixerConsts(NamedTuple):
    gmix: object
    wa: object
    wz: object
    wxbc: object
    wc: object
    wdt: object
    cwa: object
    cwb: object
    cbb: object
    dtb: object
    alog: object
    dskip: object
    gssm: object
    cwc: object
    cbc: object
    lng: object
    lnb: object
    wout: object
    e2p: object
    e2q: object


class _MixerScratch(NamedTuple):
    bufa: object
    bufb: object
    bufc: object
    st: object
    xbc: object
    dt: object
    ycat: object
    h: object
    ab: object
    z: object
    acc: object


class _SsdMasks(NamedTuple):
    tril: object
    ones2: object
    causal_rep: object
    eye_rep: object
    blockdiag: object


def _ssd_masks(q):
    hq = N_HEADS * q
    row = lax.broadcasted_iota(jnp.int32, (q, q), 0)
    col = lax.broadcasted_iota(jnp.int32, (q, q), 1)
    rowq = lax.broadcasted_iota(jnp.int32, (q, hq), 0)
    colq = lax.broadcasted_iota(jnp.int32, (q, hq), 1) % q
    heads_per_blk = MXU_DIM // q
    blk_w = heads_per_blk * HEAD_DIM
    bd_r = lax.broadcasted_iota(jnp.int32, (MXU_DIM, blk_w), 0) // q
    bd_c = lax.broadcasted_iota(jnp.int32, (MXU_DIM, blk_w), 1) // HEAD_DIM
    return _SsdMasks(tril=(col <= row).astype(BF16),
                     ones2=jnp.ones((q, 2 * q), BF16),
                     causal_rep=colq <= rowq,
                     eye_rep=(colq == rowq).astype(F32),
                     blockdiag=bd_r == bd_c)


def _group(x, g, width):
    return x[:, g * width:(g + 1) * width]


class _SsdScan:
    def __init__(self, w, s, masks, rows, n_t, q):
        self.w, self.s, self.masks = w, s, masks
        self.rows, self.q = rows, q
        self.m = rows * n_t
        self.chunks_per_row = n_t // q
        self.chunk_rows = [slice(c * q, (c + 1) * q) for c in range(self.m // q)]

    def cumsum(self):
        w, s, masks, m, q = self.w, self.s, self.masks, self.m, self.q
        a_row = -jnp.exp(w.alog[...])
        dt = s.dt[...]
        dta = dt * a_row

        def chunk_cumsum(rc):
            acs = _dot(masks.tril, jnp.concatenate(_split3(dta[rc]), axis=-1))
            return acs[:, :HEAD_PAD] + acs[:, HEAD_PAD:2 * HEAD_PAD] + acs[:, 2 * HEAD_PAD:]

        a_cum = jnp.concatenate([chunk_cumsum(rc) for rc in self.chunk_rows], axis=0)
        a_pieces = jnp.concatenate(_split2(a_cum), axis=-1)
        ex = _dot(jnp.concatenate([a_pieces, jnp.concatenate(_split2(dt), axis=-1)], axis=0),
                  w.e2p[:, :D_B])
        self.a_x, self.dt_x = ex[:m], ex[m:]
        self.a_q = self.a_x if q == HEAD_DIM else _dot(a_pieces, w.e2q[:, :N_HEADS * q])

    def decay(self):
        masks = self.masks

        def chunk_decay(rc):
            a_q = self.a_q[rc]
            a_row_b = _dot(masks.ones2, jnp.concatenate(_split2(a_q * masks.eye_rep), axis=0))
            return jnp.exp(jnp.where(masks.causal_rep, a_q - a_row_b, -jnp.inf))

        self.decays = [chunk_decay(rc) for rc in self.chunk_rows]

    def scores(self):
        xbc = self.s.xbc[...]
        self.xs = xbc[:, :D_B]
        self.bm = xbc[:, D_B:D_B + N_GROUPS * D_STATE]
        self.cm_b = xbc[:, D_B + N_GROUPS * D_STATE:].astype(BF16)
        bm_b = self.bm.astype(BF16)
        self.xdt = self.xs * self.dt_x
        self.xdt_b = self.xdt.astype(BF16)

        def chunk_scores(c, rc):
            cbt = jnp.concatenate(
                [_dot_nt(_group(self.cm_b[rc], g, D_STATE),
                         jnp.concatenate([_group(bm_b[rc], g, D_STATE)] * HEADS_PER_GROUP, axis=0))
                 for g in range(N_GROUPS)], axis=-1)
            return (cbt * self.decays[c]).astype(BF16)

        self.m_all = [chunk_scores(c, rc) for c, rc in enumerate(self.chunk_rows)]

    def within_chunk(self):
        heads_per_blk = MXU_DIM // self.q
        blk_w = heads_per_blk * HEAD_DIM

        def chunk_y(c, rc):
            parts = []
            for blk in range(N_HEADS // heads_per_blk):
                xb = _group(self.xdt_b[rc], blk, blk_w)
                rhs = jnp.where(self.masks.blockdiag, jnp.concatenate([xb] * heads_per_blk, axis=0), 0)
                parts.append(_dot(_group(self.m_all[c], blk, MXU_DIM), rhs))
            return jnp.concatenate(parts, axis=-1)

        self.y_diag = jnp.concatenate([chunk_y(c, rc) for c, rc in enumerate(self.chunk_rows)], axis=0)
        self.a_last = [self.a_x[rc.stop - 1:rc.stop, :] for rc in self.chunk_rows]

        def chunk_update(c, rc):
            xw_b = (self.xdt[rc] * jnp.exp(self.a_last[c] - self.a_x[rc])).astype(BF16)
            return jnp.concatenate(
                [_dot(_group(self.bm[rc], g, D_STATE).T.astype(BF16), _group(xw_b, g, GROUP_W))
                 for g in range(N_GROUPS)], axis=-1)

        self.upd = [chunk_update(c, rc) for c, rc in enumerate(self.chunk_rows)]

    def recur(self):
        w, s = self.w, self.s
        y_off = []
        for b in range(self.rows):
            st = s.st[b]
            for j in range(self.chunks_per_row):
                c = b * self.chunks_per_row + j
                st_b = st.astype(BF16)
                y_off.append(jnp.concatenate(
                    [_dot(_group(self.cm_b[self.chunk_rows[c]], g, D_STATE), _group(st_b, g, GROUP_W))
                     for g in range(N_GROUPS)], axis=-1))
                st = st * jnp.exp(self.a_last[c]) + self.upd[c]
            s.st[b] = st
        y = self.y_diag + jnp.concatenate(y_off, axis=0) * jnp.exp(self.a_x) + w.dskip[...] * self.xs
        yg = y * _silu(s.z[...])
        s.ycat[:, D_A:D_A + D_B] = (_rms_scale(yg) * w.gssm[...]).astype(BF16)


class _MixerChain:
    def __init__(self, x_ref, xo_ref, na_ref, nb_ref, nc_ref, w, s, masks, b0, rows, n_t, q):
        self.x_ref, self.xo_ref = x_ref, xo_ref
        self.na_ref, self.nb_ref, self.nc_ref = na_ref, nb_ref, nc_ref
        self.w, self.s, self.masks = w, s, masks
        self.b0, self.rows, self.n_t, self.q = b0, rows, n_t, q

    def norm(self):
        x = self.x_ref[self.b0:self.b0 + self.rows].reshape(self.rows * self.n_t, D_MODEL)
        self.s.h[...] = (_rms_scale(x) * self.w.gmix[...]).astype(BF16)

    def project_items(self):
        w, s, rows, n_t = self.w, self.s, self.rows, self.n_t

        def xbc():
            _conv_stage(s.bufb, _dot(s.h[...], w.wxbc[...]).reshape(rows, n_t, D_XBC), CONV_B_W, PAD_B, n_t)
            s.dt[...] = jax.nn.softplus(_dot(s.h[...], w.wdt[...]) + w.dtb[...])

        def a():
            pa = _dot(s.h[...], w.wa[...])
            s.ab[...] = pa[:, D_A:2 * D_A]
            _conv_stage(s.bufa, (pa[:, 2 * D_A:] * pa[:, :D_A]).reshape(rows, n_t, D_A),
                        CONV_A_W, PAD_A, n_t)

        def c():
            pc = _dot(s.h[...], w.wc[:, :2 * D_C])
            _conv_stage(s.bufc, (pc[:, :D_C] * jax.nn.sigmoid(pc[:, D_C:])).reshape(rows, n_t, D_C),
                        CONV_C_W, PAD_C, n_t)

        def z():
            s.z[...] = _dot(s.h[...], w.wz[:, :D_B])

        return dict(xbc=xbc, a=a, c=c, z=z)

    def conv_items(self):
        w, s, b0, rows, n_t = self.w, self.s, self.b0, self.rows, self.n_t

        def emit_b(b, r, y):
            r0 = b * n_t + r
            s.xbc[r0:r0 + y.shape[0], :] = _silu(y + w.cbb[...])

        def emit_a(b, r, y):
            r0 = b * n_t + r
            s.ycat[r0:r0 + y.shape[0], 0:D_A] = (s.ab[r0:r0 + y.shape[0], :] * y).astype(BF16)

        def emit_c(b, r, y):
            r0 = b * n_t + r
            y = y + w.cbc[...]
            mu = jnp.mean(y, axis=-1, keepdims=True)
            yc = y - mu
            var = jnp.mean(yc * yc, axis=-1, keepdims=True)
            yn = yc * lax.rsqrt(var + EPS) * w.lng[...] + w.lnb[...]
            s.ycat[r0:r0 + y.shape[0], D_A + D_B:D_MIX] = _silu(yn).astype(BF16)

        def tail_b():
            self.nb_ref[b0:b0 + rows] = _carry_tail(s.bufb, CONV_B_W, PAD_B, n_t)

        def tail_a():
            self.na_ref[b0:b0 + rows] = _carry_tail(s.bufa, CONV_A_W, PAD_A, n_t)

        def tail_c():
            self.nc_ref[b0:b0 + rows] = _carry_tail(s.bufc, CONV_C_W, PAD_C, n_t)

        return dict(
            b=_dwconv_items(s.bufb, w.cwb, CONV_B_W, PAD_B, rows, n_t, min(n_t, ROWS_B), emit_b) + [tail_b],
            a=_dwconv_items(s.bufa, w.cwa, CONV_A_W, PAD_A, rows, n_t, min(n_t, ROWS_A), emit_a) + [tail_a],
            c=_dwconv_items(s.bufc, w.cwc, CONV_C_W, PAD_C, rows, n_t, min(n_t, ROWS_C), emit_c) + [tail_c])

    def out_c(self):
        s, w = self.s, self.w
        s.acc[...] = _dot(s.ycat[:, D_A + D_B:], w.wout[D_A + D_B:, :D_MODEL])

    def out_a(self):
        s, w = self.s, self.w
        s.acc[...] += _dot(s.ycat[:, :D_A], w.wout[:D_A, :D_MODEL])

    def out(self):
        s, n_t = self.s, self.n_t
        y = s.acc[...] + _dot(s.ycat[:, D_A:D_A + D_B], self.w.wout[D_A:D_A + D_B, :D_MODEL])
        self.xo_ref[self.b0:self.b0 + self.rows] = (
            self.x_ref[self.b0:self.b0 + self.rows] + y.reshape(self.rows, n_t, D_MODEL))

    def run(self):
        proj, conv = self.project_items(), self.conv_items()
        scan = _SsdScan(self.w, self.s, self.masks, self.rows, self.n_t, self.q)
        self.norm()
        proj['xbc']()
        _run(conv['b'])
        proj['a']()
        scan.cumsum()
        scan.decay()
        proj['c']()
        scan.scores()
        proj['z']()
        _run(conv['c'])
        self.out_c()
        scan.within_chunk()
        _run(conv['a'])
        self.out_a()
        scan.recur()
        self.out()


N_MIXER_STATE_IN = 4


def _mixer_kernel(*refs, n_b, n_t, q, n_chain):
    x_ref, sta_ref, sts_ref, stb_ref, stc_ref = refs[:1 + N_MIXER_STATE_IN]
    n_in = 1 + N_MIXER_STATE_IN + len(_MixerConsts._fields)
    w = _MixerConsts(*refs[1 + N_MIXER_STATE_IN:n_in])
    xo_ref, na_ref, ns_ref, nb_ref, nc_ref = refs[n_in:n_in + 5]
    flat = refs[n_in + 5:]
    per = len(_MixerScratch._fields)
    chains = [_MixerScratch(*flat[c * per:(c + 1) * per]) for c in range(n_chain)]
    rows = n_b // n_chain
    t = pl.program_id(1)

    @pl.when(t == 0)
    def _load_state():
        for c, s in enumerate(chains):
            b0 = c * rows
            _conv_load_tail(s.bufa, sta_ref[b0:b0 + rows], CONV_A_W, PAD_A)
            _conv_load_tail(s.bufb, stb_ref[b0:b0 + rows], CONV_B_W, PAD_B)
            _conv_load_tail(s.bufc, stc_ref[b0:b0 + rows], CONV_C_W, PAD_C)
            for b in range(rows):
                s.st[b] = sts_ref[b0 + b].T

    masks = _ssd_masks(q)
    cs = [_MixerChain(x_ref, xo_ref, na_ref, nb_ref, nc_ref, w, s, masks, c * rows, rows, n_t, q)
          for c, s in enumerate(chains)]
    for chain in cs:
        chain.run()

    @pl.when(t == pl.num_programs(1) - 1)
    def _store_state():
        for c, s in enumerate(chains):
            for b in range(rows):
                ns_ref[c * rows + b] = s.st[b].T


def _expand_matrix(width):
    head = jnp.arange(HEAD_PAD)[:, None]
    lane_head = (jnp.arange(N_HEADS * width) // width)[None, :]
    e = (head == lane_head).astype(F32)
    return _mxu_weight(jnp.concatenate([e, e], axis=0))


def _const_spec(shape):
    nd = len(shape)
    return pl.BlockSpec(shape, lambda b, t: (0,) * nd, pipeline_mode=pl.Buffered(1))


def _mixer_call(x, st_a, st_ssm, st_b, st_c, w, *, n_b, n_t, n_chain):
    bsz, seq, _ = x.shape
    q = min(SSD_CHUNK, n_t)
    assert bsz % n_b == 0 and seq % n_t == 0 and n_t % q == 0 and MXU_DIM % q == 0
    assert n_b % n_chain == 0 and n_t >= CONV_C_W - 1 and n_t % SUBLANES == 0
    rows = n_b // n_chain
    m = rows * n_t
    grid = (bsz // n_b, seq // n_t)

    def bspec(shape):
        return pl.BlockSpec((n_b,) + shape, lambda b, t: (b, 0, 0))

    consts = _MixerConsts(e2p=_expand_matrix(HEAD_DIM), e2q=_expand_matrix(q),
                          **{k: w[k] for k in _MixerConsts._fields if k not in ('e2p', 'e2q')})
    in_specs = ([pl.BlockSpec((n_b, n_t, D_MODEL), lambda b, t: (b, t, 0)),
                 bspec((CONV_A_W - 1, D_A)), bspec((D_B, D_STATE)),
                 bspec((CONV_B_W - 1, D_XBC)), bspec((CONV_C_W - 1, D_C))]
                + [_const_spec(c.shape) for c in consts])
    out_shape = (jax.ShapeDtypeStruct(x.shape, F32),
                 jax.ShapeDtypeStruct(st_a.shape, F32),
                 jax.ShapeDtypeStruct(st_ssm.shape, F32),
                 jax.ShapeDtypeStruct(st_b.shape, F32),
                 jax.ShapeDtypeStruct(st_c.shape, F32))
    out_specs = (pl.BlockSpec((n_b, n_t, D_MODEL), lambda b, t: (b, t, 0)),
                 bspec((CONV_A_W - 1, D_A)), bspec((D_B, D_STATE)),
                 bspec((CONV_B_W - 1, D_XBC)), bspec((CONV_C_W - 1, D_C)))
    scratch = list(_MixerScratch(
        bufa=_conv_buf(CONV_A_W, PAD_A, rows, n_t, D_A),
        bufb=_conv_buf(CONV_B_W, PAD_B, rows, n_t, D_XBC),
        bufc=_conv_buf(CONV_C_W, PAD_C, rows, n_t, D_C),
        st=pltpu.VMEM((rows, D_STATE, D_B), F32),
        xbc=pltpu.VMEM((m, D_XBC), F32),
        dt=pltpu.VMEM((m, HEAD_PAD), F32),
        ycat=pltpu.VMEM((m, D_MIX), BF16),
        h=pltpu.VMEM((m, D_MODEL), BF16),
        ab=pltpu.VMEM((m, D_A), F32),
        z=pltpu.VMEM((m, D_B), F32),
        acc=pltpu.VMEM((m, D_MODEL), F32))) * n_chain
    return pl.pallas_call(
        functools.partial(_mixer_kernel, n_b=n_b, n_t=n_t, q=q, n_chain=n_chain),
        grid=grid, in_specs=in_specs, out_specs=out_specs, out_shape=out_shape,
        scratch_shapes=scratch,
        compiler_params=pltpu.CompilerParams(
            dimension_semantics=("arbitrary", "arbitrary"),
            vmem_limit_bytes=VMEM_LIMIT_BYTES),
        name="mixer",
    )(x, st_a, st_ssm, st_b, st_c, *consts)


class _FfnConsts(NamedTuple):
    gffn: object
    wu: object
    wg: object
    cwf: object
    wd: object
    gfin: object


class _FfnScratch(NamedTuple):
    buff: object
    hid: object


def _ffn_project(x_ref, w, s, b0, rows, n_t):
    m = rows * n_t
    x = x_ref[b0:b0 + rows].reshape(m, D_MODEL)
    h = (_rms_scale(x) * w.gffn[...]).astype(BF16)
    up = _dot(h, w.wu[...])
    _conv_stage(s.buff, _dot(h, w.wg[...]).reshape(rows, n_t, D_FF), CONV_F_W, PAD_F, n_t)
    return x, up


def _ffn_finish(x, up, xo_ref, nf_ref, w, s, b0, rows, n_t, final_norm):
    def emit_f(b, r, y):
        r0 = b * n_t + r
        s.hid[r0:r0 + y.shape[0], :] = (_silu(y) * up[r0:r0 + y.shape[0]]).astype(BF16)

    _run(_dwconv_items(s.buff, w.cwf, CONV_F_W, PAD_F, rows, n_t, ROWS_F, emit_f))
    nf_ref[b0:b0 + rows] = _carry_tail(s.buff, CONV_F_W, PAD_F, n_t)

    out = x + _dot(s.hid[...], w.wd[:, :D_MODEL])
    if final_norm:
        out = _rms_scale(out) * w.gfin[...]
    xo_ref[b0:b0 + rows] = out.reshape(rows, n_t, D_MODEL)


def _ffn_kernel(*refs, n_b, n_t, n_chain, final_norm):
    x_ref, stf_ref = refs[:2]
    n_in = 2 + len(_FfnConsts._fields)
    w = _FfnConsts(*refs[2:n_in])
    xo_ref, nf_ref = refs[n_in:n_in + 2]
    flat = refs[n_in + 2:]
    per = len(_FfnScratch._fields)
    chains = [_FfnScratch(*flat[c * per:(c + 1) * per]) for c in range(n_chain)]
    rows = n_b // n_chain

    @pl.when(pl.program_id(1) == 0)
    def _load_state():
        for c, s in enumerate(chains):
            _conv_load_tail(s.buff, stf_ref[c * rows:(c + 1) * rows], CONV_F_W, PAD_F)

    staged = [_ffn_project(x_ref, w, s, c * rows, rows, n_t) for c, s in enumerate(chains)]
    for c, s in enumerate(chains):
        x, up = staged[c]
        _ffn_finish(x, up, xo_ref, nf_ref, w, s, c * rows, rows, n_t, final_norm)


def _ffn_call(x, st_f, w, gfin, *, n_b, n_t, n_chain, final_norm):
    bsz, seq, _ = x.shape
    assert bsz % n_b == 0 and seq % n_t == 0 and n_t % SUBLANES == 0 and n_b % n_chain == 0
    rows = n_b // n_chain
    m = rows * n_t
    grid = (bsz // n_b, seq // n_t)
    consts = _FfnConsts(gfin=gfin, **{k: w[k] for k in _FfnConsts._fields if k != 'gfin'})
    in_specs = ([pl.BlockSpec((n_b, n_t, D_MODEL), lambda b, t: (b, t, 0)),
                 pl.BlockSpec((n_b, CONV_F_W - 1, D_FF), lambda b, t: (b, 0, 0))]
                + [_const_spec(c.shape) for c in consts])
    out_shape = (jax.ShapeDtypeStruct(x.shape, F32), jax.ShapeDtypeStruct(st_f.shape, F32))
    out_specs = (pl.BlockSpec((n_b, n_t, D_MODEL), lambda b, t: (b, t, 0)),
                 pl.BlockSpec((n_b, CONV_F_W - 1, D_FF), lambda b, t: (b, 0, 0)))
    scratch = list(_FfnScratch(buff=_conv_buf(CONV_F_W, PAD_F, rows, n_t, D_FF),
                               hid=pltpu.VMEM((m, D_FF), BF16))) * n_chain
    return pl.pallas_call(
        functools.partial(_ffn_kernel, n_b=n_b, n_t=n_t, n_chain=n_chain, final_norm=final_norm),
        grid=grid, in_specs=in_specs, out_specs=out_specs, out_shape=out_shape,
        scratch_shapes=scratch,
        compiler_params=pltpu.CompilerParams(
            dimension_semantics=("arbitrary", "arbitrary"),
            vmem_limit_bytes=VMEM_LIMIT_BYTES),
        name="ffn",
    )(x, st_f, *consts)


def _row(v, pad_to=None):
    v = v.astype(F32).reshape(1, -1)
    if pad_to is not None:
        v = jnp.pad(v, ((0, 0), (0, pad_to - v.shape[1])))
    return v


def _mxu_weight(w):
    w = w.astype(BF16)
    if w.shape[-1] % RELAYOUT_MINOR == 0:
        w = jnp.pad(w, ((0, 0), (0, LANES)))
    return w


def _taps(w):
    return jnp.broadcast_to(w.astype(F32)[:, None, :], (w.shape[0], SUBLANES, w.shape[1]))


def _layer_weights(i, norm_mix_g, w_in, conv_a_w, conv_b_w, conv_b_bias, dt_bias, a_log, d_skip,
                   ssm_norm_g, conv_c_w, conv_c_bias, ln_c_g, ln_c_b, w_out, norm_ffn_g, w_up,
                   conv_ffn_w, w_down):
    wi = w_in[i]
    c0 = 3 * D_A
    c1 = c0 + D_B
    c2 = c1 + D_XBC
    c3 = c2 + N_HEADS
    return dict(
        gmix=_row(norm_mix_g[i]),
        wa=_mxu_weight(wi[:, :c0]), wz=_mxu_weight(wi[:, c0:c1]),
        wxbc=_mxu_weight(wi[:, c1:c2]),
        wdt=jnp.pad(wi[:, c2:c3], ((0, 0), (0, HEAD_PAD - N_HEADS))).astype(BF16),
        wc=_mxu_weight(wi[:, c3:]),
        cwa=_taps(conv_a_w[i]), cwb=_taps(conv_b_w[i]), cbb=_row(conv_b_bias[i]),
        dtb=_row(dt_bias[i], HEAD_PAD), alog=_row(a_log[i], HEAD_PAD),
        dskip=_row(jnp.repeat(d_skip[i], HEAD_DIM)), gssm=_row(ssm_norm_g[i]),
        cwc=_taps(conv_c_w[i]), cbc=_row(conv_c_bias[i]), lng=_row(ln_c_g[i]), lnb=_row(ln_c_b[i]),
        wout=_mxu_weight(w_out[i]),
        gffn=_row(norm_ffn_g[i]),
        wu=_mxu_weight(w_up[i][:, :D_FF]), wg=_mxu_weight(w_up[i][:, D_FF:]),
        cwf=_taps(conv_ffn_w[i]), wd=_mxu_weight(w_down[i]),
    )


class _Tile(NamedTuple):
    rows: int
    tokens: int
    chains: int


class _StreamTiles(NamedTuple):
    mixer: _Tile
    ffn: _Tile


PROMPT_TILES = _StreamTiles(mixer=_Tile(1, 256, 1), ffn=_Tile(2, 256, 2))
SAMPLE_TILES = _StreamTiles(mixer=_Tile(4, 32, 1), ffn=_Tile(8, 32, 1))


def _layer(x, st_a, st_ssm, st_b, st_c, st_f, w, gfin, *, tiles, final_norm):
    bsz = x.shape[0]
    tm, tf = tiles.mixer, tiles.ffn
    x, na, ns, nb, nc = _mixer_call(x, st_a, st_ssm.reshape(bsz, D_B, D_STATE), st_b, st_c, w,
                                    n_b=tm.rows, n_t=tm.tokens, n_chain=tm.chains)
    x, nf = _ffn_call(x, st_f, w, gfin, n_b=tf.rows, n_t=tf.tokens, n_chain=tf.chains,
                      final_norm=final_norm)
    return x, (na, ns.reshape(bsz, N_HEADS, HEAD_DIM, D_STATE), nb, nc, nf)


def kernel(x_prompt, x_sample, state_conv_a, state_ssm, state_conv_b, state_conv_c, state_conv_ffn, norm_mix_g, w_in, conv_a_w, conv_b_w, conv_b_bias, dt_bias, a_log, d_skip, ssm_norm_g, conv_c_w, conv_c_bias, ln_c_g, ln_c_b, w_out, norm_ffn_g, w_up, conv_ffn_w, w_down, final_norm_g):
    depth = w_in.shape[0]
    bp = x_prompt.shape[0]
    zeros = (jnp.zeros((bp, CONV_A_W - 1, D_A), F32),
             jnp.zeros((bp, N_HEADS, HEAD_DIM, D_STATE), F32),
             jnp.zeros((bp, CONV_B_W - 1, D_XBC), F32),
             jnp.zeros((bp, CONV_C_W - 1, D_C), F32),
             jnp.zeros((bp, CONV_F_W - 1, D_FF), F32))
    gfin = _row(final_norm_g)
    hp, hs = x_prompt, x_sample
    p_states, s_states = [], []
    for i in range(depth):
        w = _layer_weights(i, norm_mix_g, w_in, conv_a_w, conv_b_w, conv_b_bias, dt_bias, a_log,
                           d_skip, ssm_norm_g, conv_c_w, conv_c_bias, ln_c_g, ln_c_b, w_out,
                           norm_ffn_g, w_up, conv_ffn_w, w_down)
        last = i == depth - 1
        hp, st = _layer(hp, *zeros, w, gfin, tiles=PROMPT_TILES, final_norm=last)
        p_states.append(st)
        hs, st = _layer(hs, state_conv_a[i], state_ssm[i], state_conv_b[i], state_conv_c[i],
                        state_conv_ffn[i], w, gfin, tiles=SAMPLE_TILES, final_norm=last)
        s_states.append(st)
    p_out = [jnp.stack([s[j] for s in p_states]) for j in range(5)]
    s_out = [jnp.stack([s[j] for s in s_states]) for j in range(5)]
    return (hp, hs, *p_out, *s_out)
```

```python
import functools
from typing import NamedTuple

import jax
import jax.numpy as jnp
from jax import lax
from jax.experimental import pallas as pl
from jax.experimental.pallas import tpu as pltpu

F32 = jnp.float32
BF16 = jnp.bfloat16

D_MODEL = 1024
D_A = 512
D_B = 1024
D_C = 512
N_HEADS = 16
HEAD_DIM = 64
N_GROUPS = 2
HEADS_PER_GROUP = N_HEADS // N_GROUPS
GROUP_W = HEADS_PER_GROUP * HEAD_DIM
D_STATE = 128
D_XBC = D_B + 2 * N_GROUPS * D_STATE
D_FF = 2816
D_MIX = D_A + D_B + D_C
CONV_A_W, CONV_B_W, CONV_C_W, CONV_F_W = 3, 4, 31, 3
EPS = 1e-5
LOG2_E = 1.4426950408889634
SSD_CHUNK = 64

LANES = 128
SUBLANES = 8
MXU_DIM = 256
HEAD_PAD = LANES
RELAYOUT_MINOR = 1024
VMEM_LIMIT_BYTES = 56 * 1024 * 1024

PAD_A, PAD_B, PAD_C, PAD_F = 8, 8, 32, 8
ROWS_A, ROWS_B, ROWS_C, ROWS_F = 32, 16, 32, 8


def _dot(a, b):
    return jnp.dot(a, b, preferred_element_type=F32)


def _dot_nt(a, b):
    return lax.dot_general(a, b, (((1,), (1,)), ((), ())), preferred_element_type=F32)


def _split2(x):
    hi = x.astype(BF16)
    return hi, (x - hi.astype(F32)).astype(BF16)


def _split3(x):
    hi = x.astype(BF16)
    r1 = x - hi.astype(F32)
    mid = r1.astype(BF16)
    return hi, mid, (r1 - mid.astype(F32)).astype(BF16)


def _silu(x):
    h = 0.5 * x
    return h + h * jnp.tanh(h)


def _rms_scale(x):
    return x * lax.rsqrt(jnp.mean(x * x, axis=-1, keepdims=True) + EPS)


def _n_shifts(k):
    return min(k, SUBLANES)


def _conv_buf(k, pad, n_b, n_t, c):
    return pltpu.VMEM((_n_shifts(k), n_b, pad + n_t + SUBLANES, c), F32)


def _conv_load_tail(buf_ref, st, k, pad):
    for v in range(_n_shifts(k)):
        buf_ref[v, :, pad - (k - 1) + v:pad + v, :] = st


def _conv_stage(buf_ref, val, k, pad, n_t):
    for v in range(_n_shifts(k)):
        buf_ref[v, :, pad + v:pad + v + n_t, :] = val


def _dwconv_items(buf_ref, w_ref, k, pad, n_b, n_t, rows, emit):
    c = buf_ref.shape[-1]

    def block(b, r):
        acc = None
        for j in range(k):
            s = pad - (k - 1) + j + r
            v = (-s) % SUBLANES
            win = buf_ref[v, b, s + v:s + v + rows, :].reshape(rows // SUBLANES, SUBLANES, c)
            term = win * w_ref[j]
            acc = term if acc is None else acc + term
        emit(b, r, acc.reshape(rows, c))

    return [functools.partial(block, b, r) for b in range(n_b) for r in range(0, n_t, rows)]


def _run(items):
    for item in items:
        item()


def _carry_tail(buf_ref, k, pad, n_t):
    tail = buf_ref[0, :, pad + n_t - (k - 1):pad + n_t, :]
    for v in range(_n_shifts(k)):
        buf_ref[v, :, pad - (k - 1) + v:pad + v, :] = buf_ref[v, :, pad + n_t - (k - 1) + v:pad + n_t + v, :]
    return tail


class _MixerConsts(NamedTuple):
    gmix: object
    wa: object
    wz: object
    wxbc: object
    wc: object
    wdt: object
    cwa: object
    cwb: object
    cbb: object
    dtb: object
    alog: object
    dskip: object
    gssm: object
    cwc: object
    cbc: object
    lng: object
    lnb: object
    wout: object
    e2p: object
    e2q: object


class _MixerScratch(NamedTuple):
    bufa: object
    bufb: object
    bufc: object
    st: object
    xbc: object
    dt: object
    ycat: object
    h: object
    ab: object
    z: object
    acc: object


class _SsdMasks(NamedTuple):
    tril: object
    ones2: object
    causal_rep: object
    eye_rep: object
    blockdiag: object


def _ssd_masks(q):
    hq = N_HEADS * q
    row = lax.broadcasted_iota(jnp.int32, (q, q), 0)
    col = lax.broadcasted_iota(jnp.int32, (q, q), 1)
    rowq = lax.broadcasted_iota(jnp.int32, (q, hq), 0)
    colq = lax.broadcasted_iota(jnp.int32, (q, hq), 1) % q
    heads_per_blk = MXU_DIM // q
    blk_w = heads_per_blk * HEAD_DIM
    bd_r = lax.broadcasted_iota(jnp.int32, (MXU_DIM, blk_w), 0) // q
    bd_c = lax.broadcasted_iota(jnp.int32, (MXU_DIM, blk_w), 1) // HEAD_DIM
    return _SsdMasks(tril=(col <= row).astype(BF16),
                     ones2=jnp.ones((q, 2 * q), BF16),
                     causal_rep=colq <= rowq,
                     eye_rep=(colq == rowq).astype(F32),
                     blockdiag=bd_r == bd_c)


def _group(x, g, width):
    return x[:, g * width:(g + 1) * width]


class _SsdScan:
    def __init__(self, w, s, masks, rows, n_t, q):
        self.w, self.s, self.masks = w, s, masks
        self.rows, self.q = rows, q
        self.m = rows * n_t
        self.chunks_per_row = n_t // q
        self.chunk_rows = [slice(c * q, (c + 1) * q) for c in range(self.m // q)]

    def cumsum(self):
        w, s, masks, m, q = self.w, self.s, self.masks, self.m, self.q
        a_row = -jnp.exp(w.alog[...]) * LOG2_E
        dt = s.dt[...]
        dta = dt * a_row

        def chunk_cumsum(rc):
            acs = _dot(masks.tril, jnp.concatenate(_split3(dta[rc]), axis=-1))
            return acs[:, :HEAD_PAD] + acs[:, HEAD_PAD:2 * HEAD_PAD] + acs[:, 2 * HEAD_PAD:]

        a_cum = jnp.concatenate([chunk_cumsum(rc) for rc in self.chunk_rows], axis=0)
        a_pieces = jnp.concatenate(_split2(a_cum), axis=-1)
        ex = _dot(jnp.concatenate([a_pieces, jnp.concatenate(_split2(dt), axis=-1)], axis=0),
                  w.e2p[:, :D_B])
        self.a_x, self.dt_x = ex[:m], ex[m:]
        self.a_q = self.a_x if q == HEAD_DIM else _dot(a_pieces, w.e2q[:, :N_HEADS * q])

    def decay(self):
        masks = self.masks

        def chunk_decay(rc):
            a_q = self.a_q[rc]
            a_row_b = _dot(masks.ones2, jnp.concatenate(_split2(a_q * masks.eye_rep), axis=0))
            return jnp.exp2(jnp.where(masks.causal_rep, a_q - a_row_b, -jnp.inf))

        self.decays = [chunk_decay(rc) for rc in self.chunk_rows]

    def scores(self):
        xbc = self.s.xbc[...]
        self.xs = xbc[:, :D_B]
        self.bm = xbc[:, D_B:D_B + N_GROUPS * D_STATE]
        self.cm_b = xbc[:, D_B + N_GROUPS * D_STATE:].astype(BF16)
        bm_b = self.bm.astype(BF16)
        self.xdt = self.xs * self.dt_x
        self.xdt_b = self.xdt.astype(BF16)

        def chunk_scores(c, rc):
            cbt = jnp.concatenate(
                [_dot_nt(_group(self.cm_b[rc], g, D_STATE),
                         jnp.concatenate([_group(bm_b[rc], g, D_STATE)] * HEADS_PER_GROUP, axis=0))
                 for g in range(N_GROUPS)], axis=-1)
            return (cbt * self.decays[c]).astype(BF16)

        self.m_all = [chunk_scores(c, rc) for c, rc in enumerate(self.chunk_rows)]

    def within_chunk(self):
        heads_per_blk = MXU_DIM // self.q
        blk_w = heads_per_blk * HEAD_DIM

        def chunk_y(c, rc):
            parts = []
            for blk in range(N_HEADS // heads_per_blk):
                xb = _group(self.xdt_b[rc], blk, blk_w)
                rhs = jnp.where(self.masks.blockdiag, jnp.concatenate([xb] * heads_per_blk, axis=0), 0)
                parts.append(_dot(_group(self.m_all[c], blk, MXU_DIM), rhs))
            return jnp.concatenate(parts, axis=-1)

        self.y_diag = jnp.concatenate([chunk_y(c, rc) for c, rc in enumerate(self.chunk_rows)], axis=0)
        self.a_last = [self.a_x[rc.stop - 1:rc.stop, :] for rc in self.chunk_rows]

        def chunk_update(c, rc):
            xw_b = (self.xdt[rc] * jnp.exp2(self.a_last[c] - self.a_x[rc])).astype(BF16)
            return jnp.concatenate(
                [_dot(_group(self.bm[rc], g, D_STATE).T.astype(BF16), _group(xw_b, g, GROUP_W))
                 for g in range(N_GROUPS)], axis=-1)

        self.upd = [chunk_update(c, rc) for c, rc in enumerate(self.chunk_rows)]

    def recur(self):
        w, s = self.w, self.s
        y_off = []
        for b in range(self.rows):
            st = s.st[b]
            for j in range(self.chunks_per_row):
                c = b * self.chunks_per_row + j
                st_b = st.astype(BF16)
                y_off.append(jnp.concatenate(
                    [_dot(_group(self.cm_b[self.chunk_rows[c]], g, D_STATE), _group(st_b, g, GROUP_W))
                     for g in range(N_GROUPS)], axis=-1))
                st = st * jnp.exp2(self.a_last[c]) + self.upd[c]
            s.st[b] = st
        y = self.y_diag + jnp.concatenate(y_off, axis=0) * jnp.exp2(self.a_x) + w.dskip[...] * self.xs
        yg = y * _silu(s.z[...])
        s.ycat[:, D_A:D_A + D_B] = (_rms_scale(yg) * w.gssm[...]).astype(BF16)


class _MixerChain:
    def __init__(self, x_ref, xo_ref, na_ref, nb_ref, nc_ref, w, s, masks, b0, rows, n_t, q):
        self.x_ref, self.xo_ref = x_ref, xo_ref
        self.na_ref, self.nb_ref, self.nc_ref = na_ref, nb_ref, nc_ref
        self.w, self.s, self.masks = w, s, masks
        self.b0, self.rows, self.n_t, self.q = b0, rows, n_t, q

    def norm(self):
        x = self.x_ref[self.b0:self.b0 + self.rows].reshape(self.rows * self.n_t, D_MODEL)
        self.s.h[...] = (_rms_scale(x) * self.w.gmix[...]).astype(BF16)

    def project_items(self):
        w, s, rows, n_t = self.w, self.s, self.rows, self.n_t

        def xbc():
            _conv_stage(s.bufb, _dot(s.h[...], w.wxbc[...]).reshape(rows, n_t, D_XBC), CONV_B_W, PAD_B, n_t)
            s.dt[...] = jax.nn.softplus(_dot(s.h[...], w.wdt[...]) + w.dtb[...])

        def a():
            pa = _dot(s.h[...], w.wa[...])
            s.ab[...] = pa[:, D_A:2 * D_A]
            _conv_stage(s.bufa, (pa[:, 2 * D_A:] * pa[:, :D_A]).reshape(rows, n_t, D_A),
                        CONV_A_W, PAD_A, n_t)

        def c():
            pc = _dot(s.h[...], w.wc[:, :2 * D_C])
            _conv_stage(s.bufc, (pc[:, :D_C] * jax.nn.sigmoid(pc[:, D_C:])).reshape(rows, n_t, D_C),
                        CONV_C_W, PAD_C, n_t)

        def z():
            s.z[...] = _dot(s.h[...], w.wz[:, :D_B])

        return dict(xbc=xbc, a=a, c=c, z=z)

    def conv_items(self):
        w, s, b0, rows, n_t = self.w, self.s, self.b0, self.rows, self.n_t

        def emit_b(b, r, y):
            r0 = b * n_t + r
            s.xbc[r0:r0 + y.shape[0], :] = _silu(y + w.cbb[...])

        def emit_a(b, r, y):
            r0 = b * n_t + r
            s.ycat[r0:r0 + y.shape[0], 0:D_A] = (s.ab[r0:r0 + y.shape[0], :] * y).astype(BF16)

        def emit_c(b, r, y):
            r0 = b * n_t + r
            y = y + w.cbc[...]
            mu = jnp.mean(y, axis=-1, keepdims=True)
            yc = y - mu
            var = jnp.mean(yc * yc, axis=-1, keepdims=True)
            yn = yc * lax.rsqrt(var + EPS) * w.lng[...] + w.lnb[...]
            s.ycat[r0:r0 + y.shape[0], D_A + D_B:D_MIX] = _silu(yn).astype(BF16)

        def tail_b():
            self.nb_ref[b0:b0 + rows] = _carry_tail(s.bufb, CONV_B_W, PAD_B, n_t)

        def tail_a():
            self.na_ref[b0:b0 + rows] = _carry_tail(s.bufa, CONV_A_W, PAD_A, n_t)

        def tail_c():
            self.nc_ref[b0:b0 + rows] = _carry_tail(s.bufc, CONV_C_W, PAD_C, n_t)

        return dict(
            b=_dwconv_items(s.bufb, w.cwb, CONV_B_W, PAD_B, rows, n_t, min(n_t, ROWS_B), emit_b) + [tail_b],
            a=_dwconv_items(s.bufa, w.cwa, CONV_A_W, PAD_A, rows, n_t, min(n_t, ROWS_A), emit_a) + [tail_a],
            c=_dwconv_items(s.bufc, w.cwc, CONV_C_W, PAD_C, rows, n_t, min(n_t, ROWS_C), emit_c) + [tail_c])

    def out_c(self):
        s, w = self.s, self.w
        s.acc[...] = _dot(s.ycat[:, D_A + D_B:], w.wout[D_A + D_B:, :D_MODEL])

    def out_a(self):
        s, w = self.s, self.w
        s.acc[...] += _dot(s.ycat[:, :D_A], w.wout[:D_A, :D_MODEL])

    def out(self):
        s, n_t = self.s, self.n_t
        y = s.acc[...] + _dot(s.ycat[:, D_A:D_A + D_B], self.w.wout[D_A:D_A + D_B, :D_MODEL])
        self.xo_ref[self.b0:self.b0 + self.rows] = (
            self.x_ref[self.b0:self.b0 + self.rows] + y.reshape(self.rows, n_t, D_MODEL))

    def run(self):
        proj, conv = self.project_items(), self.conv_items()
        scan = _SsdScan(self.w, self.s, self.masks, self.rows, self.n_t, self.q)
        self.norm()
        proj['xbc']()
        _run(conv['b'])
        proj['a']()
        scan.cumsum()
        scan.decay()
        proj['c']()
        scan.scores()
        proj['z']()
        _run(conv['c'])
        self.out_c()
        scan.within_chunk()
        _run(conv['a'])
        self.out_a()
        scan.recur()
        self.out()


N_MIXER_STATE_IN = 4


def _mixer_kernel(*refs, n_b, n_t, q, n_chain):
    x_ref, sta_ref, sts_ref, stb_ref, stc_ref = refs[:1 + N_MIXER_STATE_IN]
    n_in = 1 + N_MIXER_STATE_IN + len(_MixerConsts._fields)
    w = _MixerConsts(*refs[1 + N_MIXER_STATE_IN:n_in])
    xo_ref, na_ref, ns_ref, nb_ref, nc_ref = refs[n_in:n_in + 5]
    flat = refs[n_in + 5:]
    per = len(_MixerScratch._fields)
    chains = [_MixerScratch(*flat[c * per:(c + 1) * per]) for c in range(n_chain)]
    rows = n_b // n_chain
    t = pl.program_id(1)

    @pl.when(t == 0)
    def _load_state():
        for c, s in enumerate(chains):
            b0 = c * rows
            _conv_load_tail(s.bufa, sta_ref[b0:b0 + rows], CONV_A_W, PAD_A)
            _conv_load_tail(s.bufb, stb_ref[b0:b0 + rows], CONV_B_W, PAD_B)
            _conv_load_tail(s.bufc, stc_ref[b0:b0 + rows], CONV_C_W, PAD_C)
            for b in range(rows):
                s.st[b] = sts_ref[b0 + b].T

    masks = _ssd_masks(q)
    cs = [_MixerChain(x_ref, xo_ref, na_ref, nb_ref, nc_ref, w, s, masks, c * rows, rows, n_t, q)
          for c, s in enumerate(chains)]
    for chain in cs:
        chain.run()

    @pl.when(t == pl.num_programs(1) - 1)
    def _store_state():
        for c, s in enumerate(chains):
            for b in range(rows):
                ns_ref[c * rows + b] = s.st[b].T


def _expand_matrix(width):
    head = jnp.arange(HEAD_PAD)[:, None]
    lane_head = (jnp.arange(N_HEADS * width) // width)[None, :]
    e = (head == lane_head).astype(F32)
    return _mxu_weight(jnp.concatenate([e, e], axis=0))


def _const_spec(shape):
    nd = len(shape)
    return pl.BlockSpec(shape, lambda b, t: (0,) * nd, pipeline_mode=pl.Buffered(1))


def _mixer_call(x, st_a, st_ssm, st_b, st_c, w, *, n_b, n_t, n_chain):
    bsz, seq, _ = x.shape
    q = min(SSD_CHUNK, n_t)
    assert bsz % n_b == 0 and seq % n_t == 0 and n_t % q == 0 and MXU_DIM % q == 0
    assert n_b % n_chain == 0 and n_t >= CONV_C_W - 1 and n_t % SUBLANES == 0
    rows = n_b // n_chain
    m = rows * n_t
    grid = (bsz // n_b, seq // n_t)

    def bspec(shape):
        return pl.BlockSpec((n_b,) + shape, lambda b, t: (b, 0, 0))

    consts = _MixerConsts(e2p=_expand_matrix(HEAD_DIM), e2q=_expand_matrix(q),
                          **{k: w[k] for k in _MixerConsts._fields if k not in ('e2p', 'e2q')})
    in_specs = ([pl.BlockSpec((n_b, n_t, D_MODEL), lambda b, t: (b, t, 0)),
                 bspec((CONV_A_W - 1, D_A)), bspec((D_B, D_STATE)),
                 bspec((CONV_B_W - 1, D_XBC)), bspec((CONV_C_W - 1, D_C))]
                + [_const_spec(c.shape) for c in consts])
    out_shape = (jax.ShapeDtypeStruct(x.shape, F32),
                 jax.ShapeDtypeStruct(st_a.shape, F32),
                 jax.ShapeDtypeStruct(st_ssm.shape, F32),
                 jax.ShapeDtypeStruct(st_b.shape, F32),
                 jax.ShapeDtypeStruct(st_c.shape, F32))
    out_specs = (pl.BlockSpec((n_b, n_t, D_MODEL), lambda b, t: (b, t, 0)),
                 bspec((CONV_A_W - 1, D_A)), bspec((D_B, D_STATE)),
                 bspec((CONV_B_W - 1, D_XBC)), bspec((CONV_C_W - 1, D_C)))
    scratch = list(_MixerScratch(
        bufa=_conv_buf(CONV_A_W, PAD_A, rows, n_t, D_A),
        bufb=_conv_buf(CONV_B_W, PAD_B, rows, n_t, D_XBC),
        bufc=_conv_buf(CONV_C_W, PAD_C, rows, n_t, D_C),
        st=pltpu.VMEM((rows, D_STATE, D_B), F32),
        xbc=pltpu.VMEM((m, D_XBC), F32),
        dt=pltpu.VMEM((m, HEAD_PAD), F32),
        ycat=pltpu.VMEM((m, D_MIX), BF16),
        h=pltpu.VMEM((m, D_MODEL), BF16),
        ab=pltpu.VMEM((m, D_A), F32),
        z=pltpu.VMEM((m, D_B), F32),
        acc=pltpu.VMEM((m, D_MODEL), F32))) * n_chain
    return pl.pallas_call(
        functools.partial(_mixer_kernel, n_b=n_b, n_t=n_t, q=q, n_chain=n_chain),
        grid=grid, in_specs=in_specs, out_specs=out_specs, out_shape=out_shape,
        scratch_shapes=scratch,
        compiler_params=pltpu.CompilerParams(
            dimension_semantics=("arbitrary", "arbitrary"),
            vmem_limit_bytes=VMEM_LIMIT_BYTES),
        name="mixer",
    )(x, st_a, st_ssm, st_b, st_c, *consts)


class _FfnConsts(NamedTuple):
    gffn: object
    wu: object
    wg: object
    cwf: object
    wd: object
    gfin: object


class _FfnScratch(NamedTuple):
    buff: object
    hid: object


def _ffn_project(x_ref, w, s, b0, rows, n_t):
    m = rows * n_t
    x = x_ref[b0:b0 + rows].reshape(m, D_MODEL)
    h = (_rms_scale(x) * w.gffn[...]).astype(BF16)
    up = _dot(h, w.wu[...])
    _conv_stage(s.buff, _dot(h, w.wg[...]).reshape(rows, n_t, D_FF), CONV_F_W, PAD_F, n_t)
    return x, up


def _ffn_finish(x, up, xo_ref, nf_ref, w, s, b0, rows, n_t, final_norm):
    def emit_f(b, r, y):
        r0 = b * n_t + r
        s.hid[r0:r0 + y.shape[0], :] = (_silu(y) * up[r0:r0 + y.shape[0]]).astype(BF16)

    _run(_dwconv_items(s.buff, w.cwf, CONV_F_W, PAD_F, rows, n_t, ROWS_F, emit_f))
    nf_ref[b0:b0 + rows] = _carry_tail(s.buff, CONV_F_W, PAD_F, n_t)

    out = x + _dot(s.hid[...], w.wd[:, :D_MODEL])
    if final_norm:
        out = _rms_scale(out) * w.gfin[...]
    xo_ref[b0:b0 + rows] = out.reshape(rows, n_t, D_MODEL)


def _ffn_kernel(*refs, n_b, n_t, n_chain, final_norm):
    x_ref, stf_ref = refs[:2]
    n_in = 2 + len(_FfnConsts._fields)
    w = _FfnConsts(*refs[2:n_in])
    xo_ref, nf_ref = refs[n_in:n_in + 2]
    flat = refs[n_in + 2:]
    per = len(_FfnScratch._fields)
    chains = [_FfnScratch(*flat[c * per:(c + 1) * per]) for c in range(n_chain)]
    rows = n_b // n_chain

    @pl.when(pl.program_id(1) == 0)
    def _load_state():
        for c, s in enumerate(chains):
            _conv_load_tail(s.buff, stf_ref[c * rows:(c + 1) * rows], CONV_F_W, PAD_F)

    staged = [_ffn_project(x_ref, w, s, c * rows, rows, n_t) for c, s in enumerate(chains)]
    for c, s in enumerate(chains):
        x, up = staged[c]
        _ffn_finish(x, up, xo_ref, nf_ref, w, s, c * rows, rows, n_t, final_norm)


def _ffn_call(x, st_f, w, gfin, *, n_b, n_t, n_chain, final_norm):
    bsz, seq, _ = x.shape
    assert bsz % n_b == 0 and seq % n_t == 0 and n_t % SUBLANES == 0 and n_b % n_chain == 0
    rows = n_b // n_chain
    m = rows * n_t
    grid = (bsz // n_b, seq // n_t)
    consts = _FfnConsts(gfin=gfin, **{k: w[k] for k in _FfnConsts._fields if k != 'gfin'})
    in_specs = ([pl.BlockSpec((n_b, n_t, D_MODEL), lambda b, t: (b, t, 0)),
                 pl.BlockSpec((n_b, CONV_F_W - 1, D_FF), lambda b, t: (b, 0, 0))]
                + [_const_spec(c.shape) for c in consts])
    out_shape = (jax.ShapeDtypeStruct(x.shape, F32), jax.ShapeDtypeStruct(st_f.shape, F32))
    out_specs = (pl.BlockSpec((n_b, n_t, D_MODEL), lambda b, t: (b, t, 0)),
                 pl.BlockSpec((n_b, CONV_F_W - 1, D_FF), lambda b, t: (b, 0, 0)))
    scratch = list(_FfnScratch(buff=_conv_buf(CONV_F_W, PAD_F, rows, n_t, D_FF),
                               hid=pltpu.VMEM((m, D_FF), BF16))) * n_chain
    return pl.pallas_call(
        functools.partial(_ffn_kernel, n_b=n_b, n_t=n_t, n_chain=n_chain, final_norm=final_norm),
        grid=grid, in_specs=in_specs, out_specs=out_specs, out_shape=out_shape,
        scratch_shapes=scratch,
        compiler_params=pltpu.CompilerParams(
            dimension_semantics=("arbitrary", "arbitrary"),
            vmem_limit_bytes=VMEM_LIMIT_BYTES),
        name="ffn",
    )(x, st_f, *consts)


def _row(v, pad_to=None):
    v = v.astype(F32).reshape(1, -1)
    if pad_to is not None:
        v = jnp.pad(v, ((0, 0), (0, pad_to - v.shape[1])))
    return v


def _mxu_weight(w):
    w = w.astype(BF16)
    if w.shape[-1] % RELAYOUT_MINOR == 0:
        w = jnp.concatenate([w, jnp.zeros((w.shape[0], LANES), BF16)], axis=1)
    return w


def _taps(w):
    return jnp.broadcast_to(w.astype(F32)[:, None, :], (w.shape[0], SUBLANES, w.shape[1]))


def _layer_weights(i, norm_mix_g, w_in, conv_a_w, conv_b_w, conv_b_bias, dt_bias, a_log, d_skip,
                   ssm_norm_g, conv_c_w, conv_c_bias, ln_c_g, ln_c_b, w_out, norm_ffn_g, w_up,
                   conv_ffn_w, w_down):
    wi = w_in[i]
    c0 = 3 * D_A
    c1 = c0 + D_B
    c2 = c1 + D_XBC
    c3 = c2 + N_HEADS
    return dict(
        gmix=_row(norm_mix_g[i]),
        wa=_mxu_weight(wi[:, :c0]), wz=_mxu_weight(wi[:, c0:c1]),
        wxbc=_mxu_weight(wi[:, c1:c2]),
        wdt=jnp.pad(wi[:, c2:c3], ((0, 0), (0, HEAD_PAD - N_HEADS))).astype(BF16),
        wc=_mxu_weight(wi[:, c3:]),
        cwa=_taps(conv_a_w[i]), cwb=_taps(conv_b_w[i]), cbb=_row(conv_b_bias[i]),
        dtb=_row(dt_bias[i], HEAD_PAD), alog=_row(a_log[i], HEAD_PAD),
        dskip=_row(jnp.repeat(d_skip[i], HEAD_DIM)), gssm=_row(ssm_norm_g[i]),
        cwc=_taps(conv_c_w[i]), cbc=_row(conv_c_bias[i]), lng=_row(ln_c_g[i]), lnb=_row(ln_c_b[i]),
        wout=_mxu_weight(w_out[i]),
        gffn=_row(norm_ffn_g[i]),
        wu=_mxu_weight(w_up[i][:, :D_FF]), wg=_mxu_weight(w_up[i][:, D_FF:]),
        cwf=_taps(conv_ffn_w[i]), wd=_mxu_weight(w_down[i]),
    )


class _Tile(NamedTuple):
    rows: int
    tokens: int
    chains: int


class _StreamTiles(NamedTuple):
    mixer: _Tile
    ffn: _Tile


PROMPT_TILES = _StreamTiles(mixer=_Tile(1, 256, 1), ffn=_Tile(2, 256, 2))
SAMPLE_TILES = _StreamTiles(mixer=_Tile(4, 32, 1), ffn=_Tile(8, 32, 1))


def _layer(x, st_a, st_ssm, st_b, st_c, st_f, w, gfin, *, tiles, final_norm):
    bsz = x.shape[0]
    tm, tf = tiles.mixer, tiles.ffn
    x, na, ns, nb, nc = _mixer_call(x, st_a, st_ssm.reshape(bsz, D_B, D_STATE), st_b, st_c, w,
                                    n_b=tm.rows, n_t=tm.tokens, n_chain=tm.chains)
    x, nf = _ffn_call(x, st_f, w, gfin, n_b=tf.rows, n_t=tf.tokens, n_chain=tf.chains,
                      final_norm=final_norm)
    return x, (na, ns.reshape(bsz, N_HEADS, HEAD_DIM, D_STATE), nb, nc, nf)


def kernel(x_prompt, x_sample, state_conv_a, state_ssm, state_conv_b, state_conv_c, state_conv_ffn, norm_mix_g, w_in, conv_a_w, conv_b_w, conv_b_bias, dt_bias, a_log, d_skip, ssm_norm_g, conv_c_w, conv_c_bias, ln_c_g, ln_c_b, w_out, norm_ffn_g, w_up, conv_ffn_w, w_down, final_norm_g):
    depth = w_in.shape[0]
    bp = x_prompt.shape[0]
    zeros = (jnp.zeros((bp, CONV_A_W - 1, D_A), F32),
             jnp.zeros((bp, N_HEADS, HEAD_DIM, D_STATE), F32),
             jnp.zeros((bp, CONV_B_W - 1, D_XBC), F32),
             jnp.zeros((bp, CONV_C_W - 1, D_C), F32),
             jnp.zeros((bp, CONV_F_W - 1, D_FF), F32))
    gfin = _row(final_norm_g)
    hp, hs = x_prompt, x_sample
    p_states, s_states = [], []
    for i in range(depth):
        w = _layer_weights(i, norm_mix_g, w_in, conv_a_w, conv_b_w, conv_b_bias, dt_bias, a_log,
                           d_skip, ssm_norm_g, conv_c_w, conv_c_bias, ln_c_g, ln_c_b, w_out,
                           norm_ffn_g, w_up, conv_ffn_w, w_down)
        last = i == depth - 1
        hp, st = _layer(hp, *zeros, w, gfin, tiles=PROMPT_TILES, final_norm=last)
        p_states.append(st)
        hs, st = _layer(hs, state_conv_a[i], state_ssm[i], state_conv_b[i], state_conv_c[i],
                        state_conv_ffn[i], w, gfin, tiles=SAMPLE_TILES, final_norm=last)
        s_states.append(st)
    p_out = [jnp.stack([s[j] for s in p_states]) for j in range(5)]
    s_out = [jnp.stack([s[j] for s in s_states]) for j in range(5)]
    return (hp, hs, *p_out, *s_out)
```
